```python
import jax, jax.numpy as jnp
from jax import lax
import numpy as np

D_MODEL = 1024
BATCH = 4
SEQ = 8192
DEPTH = 4
DEC_BATCH = 16
DEC_SEQ = 32
PAST_LEN = 2048

CHUNK = 64
N_EVEN = (DEPTH + 1) // 2
N_ODD = DEPTH // 2
EPS = 1e-6

A_CHUNK = 128
A_GROUPS = 4
A_WIDTH = D_MODEL
A_GROUP_DIM = A_WIDTH // A_GROUPS
B_HEAD_DIM = 64
B_WIDTH = D_MODEL
B_HEADS = B_WIDTH // B_HEAD_DIM
B_GROUPS = 2
B_STATE = 128
B_CONV = 4
B_CONV_DIM = B_WIDTH + 2 * B_GROUPS * B_STATE
SSD_BLOCK = 128
C_HEADS = 8
C_HEAD_DIM = D_MODEL // C_HEADS
C_WIDTH = C_HEADS * C_HEAD_DIM
Q_BLOCK = 128

EVEN_IN = 3 * A_WIDTH + B_WIDTH + B_CONV_DIM + B_HEADS
EVEN_SPLITS = (A_WIDTH, 2 * A_WIDTH, 3 * A_WIDTH, 3 * A_WIDTH + B_WIDTH, 3 * A_WIDTH + B_WIDTH + B_CONV_DIM)
ODD_IN = 4 * C_WIDTH + C_HEADS
ODD_SPLITS = (C_WIDTH, 2 * C_WIDTH, 3 * C_WIDTH, 4 * C_WIDTH)

kernel_name = "hybrid_stream_gmlp_ssd_fox_step"


def rmsnorm(x, g):
    xf = x.astype(jnp.float32)
    y = xf * lax.rsqrt(jnp.mean(xf * xf, axis=-1, keepdims=True) + EPS) * g.astype(jnp.float32)
    return y.astype(x.dtype)


def causal_dwconv(x, ctx, w, b):
    L = x.shape[1]
    xp = jnp.concatenate([ctx.astype(x.dtype), x], axis=1)
    y = b.astype(x.dtype) + sum(xp[:, k:k + L] * w[:, k].astype(x.dtype) for k in range(B_CONV))
    return y, xp[:, xp.shape[1] - (B_CONV - 1):]


def gmlp_branch(u, v, z, ws, bs, gv):
    b, L, _ = u.shape
    u = jax.nn.gelu(u, approximate=False)
    v = rmsnorm(jax.nn.gelu(v, approximate=False).reshape(b, L, A_GROUPS, A_GROUP_DIM), gv)
    n = min(L, A_CHUNK)
    nc = L // n
    pos = jnp.arange(n)
    mask = (pos[None, :] // CHUNK) <= (pos[:, None] // CHUNK)
    w = jnp.where(mask[None], ws[:, :n, :n], 0.0).astype(v.dtype)
    s = jnp.einsum('gij,bcjgk->bcigk', w, v.reshape(b, nc, n, A_GROUPS, A_GROUP_DIM))
    s = s + bs[:, :n].T[None, None, :, :, None].astype(s.dtype)
    s = s.reshape(b, L, A_WIDTH)
    return jax.nn.silu(z) * u * s, v.reshape(b, L, A_WIDTH)


def ssd_scan(x, dt, a, bm, cm, s0, blk):
    b, L, H, P = x.shape
    G, N = bm.shape[2], bm.shape[3]
    R = H // G
    nb = L // blk
    x = x.reshape(b, nb, blk, G, R, P)
    dt = dt.reshape(b, nb, blk, G, R)
    acs = jnp.cumsum(dt * a.reshape(G, R), axis=2)
    bm = bm.reshape(b, nb, blk, G, N)
    cm = cm.reshape(b, nb, blk, G, N)
    dtx = dt[..., None] * x
    causal = jnp.tril(jnp.ones((blk, blk), dtype=bool))
    seg = acs[:, :, :, None] - acs[:, :, None, :]
    decay = jnp.exp(jnp.where(causal[:, :, None, None], seg, -jnp.inf))
    cb = jnp.einsum('bclgn,bcsgn->bclsg', cm, bm)
    y_intra = jnp.einsum('bclsg,bclsgr,bcsgrp->bclgrp', cb, decay, dtx)
    to_end = jnp.exp(acs[:, :, -1:] - acs)
    ds = jnp.einsum('bclgr,bclgrp,bclgn->bcgrpn', to_end, dtx, bm)
    blk_decay = jnp.exp(acs[:, :, -1])

    def step(s, inp):
        ds_c, dec_c = inp
        return dec_c[..., None, None] * s + ds_c, s

    s_fin, s_prev = lax.scan(step, s0.reshape(b, G, R, P, N),
                             (jnp.moveaxis(ds, 1, 0), jnp.moveaxis(blk_decay, 1, 0)))
    s_prev = jnp.moveaxis(s_prev, 0, 1)
    y_state = jnp.einsum('bclgn,bcgrpn,bclgr->bclgrp', cm, s_prev, jnp.exp(acs))
    return (y_intra + y_state).reshape(b, L, H, P), s_fin.reshape(b, H, P, N)


def ssd_branch(zb, xbc, dt_raw, conv_ctx, s0, conv_w, conv_b, dt_bias, a_log, d_skip, g_ssd):
    b, L, _ = xbc.shape
    xbc, new_ctx = causal_dwconv(xbc, conv_ctx, conv_w, conv_b)
    xbc = jax.nn.silu(xbc).astype(jnp.float32)
    xs = xbc[..., :B_WIDTH].reshape(b, L, B_HEADS, B_HEAD_DIM)
    bm = xbc[..., B_WIDTH:B_WIDTH + B_GROUPS * B_STATE].reshape(b, L, B_GROUPS, B_STATE)
    cm = xbc[..., B_WIDTH + B_GROUPS * B_STATE:].reshape(b, L, B_GROUPS, B_STATE)
    dt = jax.nn.softplus(dt_raw.astype(jnp.float32) + dt_bias.astype(jnp.float32))
    a = -jnp.exp(a_log.astype(jnp.float32))
    blk = SSD_BLOCK if L % SSD_BLOCK == 0 else L
    y, s_fin = ssd_scan(xs, dt, a, bm, cm, s0.astype(jnp.float32), blk)
    y = y + d_skip.astype(jnp.float32)[:, None] * xs
    y = y.reshape(b, L, B_WIDTH) * jax.nn.silu(zb.astype(jnp.float32))
    y = rmsnorm(y.reshape(b, L, B_GROUPS, B_WIDTH // B_GROUPS),
                g_ssd.reshape(B_GROUPS, B_WIDTH // B_GROUPS)).reshape(b, L, B_WIDTH)
    return y.astype(zb.dtype), s_fin, new_ctx


def fox_branch(q, k, v, z, f_logit, past_k, past_v, past_logf, gq, gk):
    b, L, _ = q.shape
    q = rmsnorm(q.reshape(b, L, C_HEADS, C_HEAD_DIM), gq)
    k = rmsnorm(k.reshape(b, L, C_HEADS, C_HEAD_DIM), gk)
    v = v.reshape(b, L, C_HEADS, C_HEAD_DIM)
    logf = jax.nn.log_sigmoid(f_logit.astype(jnp.float32))
    p0 = past_k.shape[1]
    k_all = jnp.concatenate([past_k.astype(k.dtype), k], axis=1)
    v_all = jnp.concatenate([past_v.astype(v.dtype), v], axis=1)
    f_cum = jnp.cumsum(jnp.concatenate([past_logf.astype(jnp.float32), logf], axis=1), axis=1)
    f_k = jnp.moveaxis(f_cum, 2, 1)
    key_pos = jnp.arange(p0 + L)
    q_pos = p0 + jnp.arange(L)
    qb = Q_BLOCK if L % Q_BLOCK == 0 else L
    nb = L // qb
    scale = C_HEAD_DIM ** -0.5

    def attend(args):
        q_blk, fq_blk, pos_blk = args
        s = jnp.einsum('blhd,bshd->bhls', q_blk, k_all).astype(jnp.float32) * scale
        s = s + jnp.moveaxis(fq_blk, 2, 1)[..., None] - f_k[:, :, None, :]
        s = jnp.where(key_pos[None, :] <= pos_blk[:, None], s, -jnp.inf)
        p = jax.nn.softmax(s, axis=-1).astype(v_all.dtype)
        return jnp.einsum('bhls,bshd->blhd', p, v_all)

    out = lax.map(attend, (jnp.moveaxis(q.reshape(b, nb, qb, C_HEADS, C_HEAD_DIM), 1, 0),
                           jnp.moveaxis(f_cum[:, p0:].reshape(b, nb, qb, C_HEADS), 1, 0),
                           q_pos.reshape(nb, qb)))
    out = jnp.moveaxis(out, 0, 1).reshape(b, L, C_WIDTH)
    return jax.nn.silu(z) * out, k, v, logf


def even_layer(x, g_pre, g_post, w_in, w_out, ws, bs, gv, conv_w, conv_b, dt_bias, a_log, d_skip, g_ssd,
               conv_ctx, s0):
    h = rmsnorm(x, g_pre)
    proj = jnp.einsum('bld,de->ble', h, w_in)
    u, v, za, zb, xbc, dt_raw = jnp.split(proj, EVEN_SPLITS, axis=-1)
    ya, v_rows = gmlp_branch(u, v, za, ws, bs, gv)
    yb, s_fin, new_ctx = ssd_branch(zb, xbc, dt_raw, conv_ctx, s0, conv_w, conv_b, dt_bias, a_log, d_skip, g_ssd)
    o = jnp.einsum('ble,ed->bld', jnp.concatenate([ya, yb.astype(ya.dtype)], axis=-1), w_out)
    return x + rmsnorm(o, g_post), v_rows, s_fin, new_ctx


def odd_layer(x, g_pre, g_post, w_in, b_f, w_out, gq, gk, past_k, past_v, past_logf):
    h = rmsnorm(x, g_pre)
    proj = jnp.einsum('bld,de->ble', h, w_in)
    q, k, v, z, f_logit = jnp.split(proj, ODD_SPLITS, axis=-1)
    yc, k_rows, v_rows, logf_rows = fox_branch(q, k, v, z, f_logit + b_f.astype(f_logit.dtype),
                                               past_k, past_v, past_logf, gq, gk)
    o = jnp.einsum('ble,ed->bld', yc, w_out)
    return x + rmsnorm(o, g_post), k_rows, v_rows, logf_rows


def setup_inputs(seed: int = 0) -> dict:
    key = jax.random.key(seed)
    ks = jax.random.split(key, 32)
    f32 = jnp.float32
    nrm = lambda k, shape, s=1.0: s * jax.random.normal(k, shape, f32)
    dt0 = jnp.exp(jax.random.uniform(ks[0], (N_EVEN, B_HEADS), f32) * (np.log(0.1) - np.log(0.001)) + np.log(0.001))
    return {
        "x_prompt": nrm(ks[1], (BATCH, SEQ, D_MODEL)),
        "x_sample": nrm(ks[2], (DEC_BATCH, DEC_SEQ, D_MODEL)),
        "cache_fox_k": nrm(ks[3], (N_ODD, DEC_BATCH, PAST_LEN, C_HEADS, C_HEAD_DIM)),
        "cache_fox_v": nrm(ks[4], (N_ODD, DEC_BATCH, PAST_LEN, C_HEADS, C_HEAD_DIM)),
        "cache_fox_logf": jax.nn.log_sigmoid(2.0 + nrm(ks[5], (N_ODD, DEC_BATCH, PAST_LEN, C_HEADS), 0.5)),
        "state_ssd": nrm(ks[6], (N_EVEN, DEC_BATCH, B_HEADS, B_HEAD_DIM, B_STATE), 0.1),
        "state_conv": nrm(ks[7], (N_EVEN, DEC_BATCH, B_CONV - 1, B_CONV_DIM)),
        "norm_pre": 1.0 + nrm(ks[8], (DEPTH, D_MODEL), 0.1),
        "norm_post": 1.0 + nrm(ks[9], (DEPTH, D_MODEL), 0.1),
        "w_in_even": nrm(ks[10], (N_EVEN, D_MODEL, EVEN_IN), D_MODEL ** -0.5),
        "w_out_even": nrm(ks[11], (N_EVEN, A_WIDTH + B_WIDTH, D_MODEL), (A_WIDTH + B_WIDTH) ** -0.5),
        "gmlp_ws": nrm(ks[12], (N_EVEN, A_GROUPS, A_CHUNK, A_CHUNK), A_CHUNK ** -0.5),
        "gmlp_bs": 1.0 + nrm(ks[13], (N_EVEN, A_GROUPS, A_CHUNK), 0.1),
        "gmlp_gv": 1.0 + nrm(ks[14], (N_EVEN, A_GROUPS, A_GROUP_DIM), 0.1),
        "ssd_conv_w": nrm(ks[15], (N_EVEN, B_CONV_DIM, B_CONV), B_CONV ** -0.5),
        "ssd_conv_b": nrm(ks[16], (N_EVEN, B_CONV_DIM), 0.01),
        "ssd_dt_bias": dt0 + jnp.log(-jnp.expm1(-dt0)),
        "ssd_a_log": jnp.log(jax.random.uniform(ks[17], (N_EVEN, B_HEADS), f32, 1.0, 16.0)),
        "ssd_d": 1.0 + nrm(ks[18], (N_EVEN, B_HEADS), 0.1),
        "ssd_norm_g": 1.0 + nrm(ks[19], (N_EVEN, B_WIDTH), 0.1),
        "w_in_odd": nrm(ks[20], (N_ODD, D_MODEL, ODD_IN), D_MODEL ** -0.5),
        "fox_b_forget": 2.0 + nrm(ks[21], (N_ODD, C_HEADS), 0.5),
        "w_out_odd": nrm(ks[22], (N_ODD, C_WIDTH, D_MODEL), C_WIDTH ** -0.5),
        "fox_gq": 1.0 + nrm(ks[23], (N_ODD, C_HEAD_DIM), 0.1),
        "fox_gk": 1.0 + nrm(ks[24], (N_ODD, C_HEAD_DIM), 0.1),
    }


def reference(x_prompt, x_sample, cache_fox_k, cache_fox_v, cache_fox_logf, state_ssd, state_conv,
              norm_pre, norm_post, w_in_even, w_out_even, gmlp_ws, gmlp_bs, gmlp_gv,
              ssd_conv_w, ssd_conv_b, ssd_dt_bias, ssd_a_log, ssd_d, ssd_norm_g,
              w_in_odd, fox_b_forget, w_out_odd, fox_gq, fox_gk):
    xp, xs = x_prompt, x_sample
    bp = xp.shape[0]
    kp_l, vp_l, lp_l, sp_l, cp_l = [], [], [], [], []
    ks_l, vs_l, ls_l, ss_l, cs_l, gs_l = [], [], [], [], [], []
    for i in range(DEPTH):
        j = i // 2
        if i % 2 == 0:
            ew = (w_in_even[j], w_out_even[j], gmlp_ws[j], gmlp_bs[j], gmlp_gv[j], ssd_conv_w[j], ssd_conv_b[j],
                  ssd_dt_bias[j], ssd_a_log[j], ssd_d[j], ssd_norm_g[j])
            xp, _, sp, cp = even_layer(xp, norm_pre[i], norm_post[i], *ew,
                                       jnp.zeros((bp, B_CONV - 1, B_CONV_DIM), xp.dtype),
                                       jnp.zeros((bp, B_HEADS, B_HEAD_DIM, B_STATE), jnp.float32))
            xs, gv_s, ss, cs = even_layer(xs, norm_pre[i], norm_post[i], *ew, state_conv[j], state_ssd[j])
            sp_l.append(sp); cp_l.append(cp)
            ss_l.append(ss); cs_l.append(cs); gs_l.append(gv_s)
        else:
            ow = (w_in_odd[j], fox_b_forget[j], w_out_odd[j], fox_gq[j], fox_gk[j])
            xp, kp, vp, lp = odd_layer(xp, norm_pre[i], norm_post[i], *ow,
                                       jnp.zeros((bp, 0, C_HEADS, C_HEAD_DIM), xp.dtype),
                                       jnp.zeros((bp, 0, C_HEADS, C_HEAD_DIM), xp.dtype),
                                       jnp.zeros((bp, 0, C_HEADS), jnp.float32))
            xs, ksr, vsr, lsr = odd_layer(xs, norm_pre[i], norm_post[i], *ow,
                                          cache_fox_k[j], cache_fox_v[j], cache_fox_logf[j])
            kp_l.append(kp); vp_l.append(vp); lp_l.append(lp)
            ks_l.append(ksr); vs_l.append(vsr); ls_l.append(lsr)
    fox_k_prompt = jnp.stack(kp_l)
    fox_v_prompt = jnp.stack(vp_l)
    fox_logf_prompt = jnp.stack(lp_l)
    ssd_state_prompt = jnp.stack(sp_l)
    conv_state_prompt = jnp.stack(cp_l)
    fox_k_sample = jnp.stack(ks_l)
    fox_v_sample = jnp.stack(vs_l)
    fox_logf_sample = jnp.stack(ls_l)
    ssd_state_sample = jnp.stack(ss_l)
    conv_state_sample = jnp.stack(cs_l)
    gmlp_v_sample = jnp.stack(gs_l)
    return (xp, xs, fox_k_prompt, fox_v_prompt, fox_logf_prompt, ssd_state_prompt, conv_state_prompt,
            fox_k_sample, fox_v_sample, fox_logf_sample, ssd_state_sample, conv_state_sample, gmlp_v_sample)
```

```python
import functools

import numpy as np
import jax
import jax.numpy as jnp
from jax import lax
from jax.experimental import pallas as pl
from jax.experimental.pallas import tpu as pltpu

F32 = jnp.float32
BF16 = jnp.bfloat16
EPS = 1e-6
LANES = 128
VMEM_LIMIT = 56 * 1024 * 1024

D_MODEL = 1024
A_GROUPS = 4
A_GROUP_DIM = 256
A_CHUNK = 128
SUB_CHUNK = 64
B_HEADS = 16
B_HEAD_DIM = 64
B_GROUPS = 2
B_STATE = 128
B_CONV = 4
B_WIDTH = 1024
B_CONV_DIM = B_WIDTH + 2 * B_GROUPS * B_STATE
CTX_ROW = 8
C_HEADS = 8
C_HEAD_DIM = 128
SQRT_HALF = 0.7071067811865476


def _params(sem):
    return pltpu.CompilerParams(dimension_semantics=sem, vmem_limit_bytes=VMEM_LIMIT)


def _const_spec(shape):
    nd = len(shape)
    return pl.BlockSpec(shape, lambda *_: (0,) * nd)


def _dot(a, b):
    return jnp.dot(a, b, preferred_element_type=F32)


def _dot_nt(a, b):
    return lax.dot_general(a, b, (((1,), (1,)), ((), ())), preferred_element_type=F32)


def _split3(x):
    hi = x.astype(BF16)
    r = x - hi.astype(F32)
    mid = r.astype(BF16)
    lo = (r - mid.astype(F32)).astype(BF16)
    return hi, mid, lo


def _dot3_lhs(x, w):
    hi, mid, lo = _split3(x)
    return _dot(hi, w) + _dot(mid, w) + _dot(lo, w)


def _dot3_rhs(w, x):
    hi, mid, lo = _split3(x)
    return _dot(w, hi) + _dot(w, mid) + _dot(w, lo)


def _rms(x, g):
    return x * lax.rsqrt(jnp.mean(x * x, axis=-1, keepdims=True) + EPS) * g


def _gelu(x):
    return 0.5 * x * (1.0 + lax.erf(x * SQRT_HALF))


def _silu(x):
    return x * jax.nn.sigmoid(x)


def _norm_proj_body(x_ref, g_ref, *refs):
    n = len(refs) // 2
    h = _rms(x_ref[...], g_ref[...]).astype(BF16)
    for w_ref, o_ref in zip(refs[:n], refs[n:]):
        o_ref[...] = _dot(h, w_ref[...])


def norm_proj(x, g, ws, tm):
    t, d = x.shape
    return pl.pallas_call(
        _norm_proj_body,
        grid=(t // tm,),
        in_specs=[pl.BlockSpec((tm, d), lambda i: (i, 0)), _const_spec((1, d))]
        + [_const_spec(w.shape) for w in ws],
        out_specs=[pl.BlockSpec((tm, w.shape[1]), lambda i: (i, 0)) for w in ws],
        out_shape=[jax.ShapeDtypeStruct((t, w.shape[1]), F32) for w in ws],
        compiler_params=_params(("parallel",)),
        name="norm_proj",
    )(x, g, *ws)


def _out_proj_body(y_ref, w_ref, x_ref, g_ref, o_ref):
    o = _dot(y_ref[...], w_ref[...])
    o_ref[...] = x_ref[...] + _rms(o, g_ref[...])


def out_proj(y, w, x, g, tm):
    t, d = x.shape
    kd = y.shape[1]
    return pl.pallas_call(
        _out_proj_body,
        grid=(t // tm,),
        in_specs=[pl.BlockSpec((tm, kd), lambda i: (i, 0)), _const_spec(w.shape),
                  pl.BlockSpec((tm, d), lambda i: (i, 0)), _const_spec((1, d))],
        out_specs=pl.BlockSpec((tm, d), lambda i: (i, 0)),
        out_shape=jax.ShapeDtypeStruct((t, d), F32),
        compiler_params=_params(("parallel",)),
        name="out_proj",
    )(y, w, x, g)


def _even_mixer_body(u_ref, v_ref, za_ref, zb_ref, xbc_ref, dt_ref,
                     ws_ref, bs_ref, gv_ref, cw_ref, cb_ref, dtb_ref, alog_ref, dskip_ref, gssd_ref,
                     e_ref, conv0_ref, s0_ref,
                     y_ref, vout_ref, sfin_ref, cfin_ref,
                     xpad_sc, st_sc, *, blk):
    j = pl.program_id(1)
    last = pl.num_programs(1) - 1
    row = lax.broadcasted_iota(jnp.int32, (blk, blk), 0)
    col = lax.broadcasted_iota(jnp.int32, (blk, blk), 1)

    gu = _gelu(u_ref[...])
    gv = _gelu(v_ref[...])
    vn = jnp.concatenate(
        [_rms(gv[:, g * A_GROUP_DIM:(g + 1) * A_GROUP_DIM], gv_ref[g:g + 1, :]) for g in range(A_GROUPS)],
        axis=1)
    vout_ref[...] = vn
    chunk_causal = (col // SUB_CHUNK) <= (row // SUB_CHUNK)
    s_parts = []
    for g in range(A_GROUPS):
        wg = jnp.where(chunk_causal, ws_ref[g], 0.0).astype(BF16)
        sg = _dot(wg, vn[:, g * A_GROUP_DIM:(g + 1) * A_GROUP_DIM].astype(BF16))
        s_parts.append(sg + bs_ref[:, g:g + 1])
    ya = _silu(za_ref[...]) * gu * jnp.concatenate(s_parts, axis=1)

    @pl.when(j == 0)
    def _():
        xpad_sc[CTX_ROW - (B_CONV - 1):CTX_ROW, :] = conv0_ref[...]
        st_sc[...] = s0_ref[...].T

    xpad_sc[CTX_ROW:CTX_ROW + blk, :] = xbc_ref[...]
    acc = cb_ref[...]
    for k in range(B_CONV):
        r0 = CTX_ROW - (B_CONV - 1) + k
        acc = acc + xpad_sc[r0:r0 + blk, :] * cw_ref[k:k + 1, :]
    xc = _silu(acc)
    new_ctx = xpad_sc[CTX_ROW + blk - (B_CONV - 1):CTX_ROW + blk, :]
    xpad_sc[CTX_ROW - (B_CONV - 1):CTX_ROW, :] = new_ctx

    @pl.when(j == last)
    def _():
        cfin_ref[...] = new_ctx

    xs = xc[:, :B_WIDTH]
    bm = xc[:, B_WIDTH:B_WIDTH + B_GROUPS * B_STATE]
    cm = xc[:, B_WIDTH + B_GROUPS * B_STATE:]

    lane = lax.broadcasted_iota(jnp.int32, (1, LANES), 1)
    dt = jax.nn.softplus(dt_ref[...] + dtb_ref[...])
    a = jnp.where(lane < B_HEADS, -jnp.exp(alog_ref[...]), 0.0)
    tril = (col <= row).astype(BF16)
    acs = _dot3_rhs(tril, dt * a)
    acs_t = acs.T
    acs_last = acs[blk - 1:blk, :]
    e = e_ref[...]
    dt_e = _dot3_lhs(dt, e)
    expacs_e = _dot3_lhs(jnp.exp(acs), e)
    toend_e = _dot3_lhs(jnp.exp(acs_last - acs), e)
    blkdec_e = expacs_e[blk - 1:blk, :]
    dtx = dt_e * xs
    wds = (toend_e * dtx).astype(BF16)
    dtx_b = dtx.astype(BF16)
    lane_hp = lax.broadcasted_iota(jnp.int32, (blk, LANES), 1)
    gw = (B_HEADS // B_GROUPS) * B_HEAD_DIM
    ys_parts, yi_parts = [], []
    for g in range(B_GROUPS):
        bm_g = bm[:, g * B_STATE:(g + 1) * B_STATE]
        cm_g = cm[:, g * B_STATE:(g + 1) * B_STATE].astype(BF16)
        cb = _dot_nt(cm_g, bm_g.astype(BF16))
        st_g = st_sc[:, g * gw:(g + 1) * gw]
        ds = _dot(bm_g.T.astype(BF16), wds[:, g * gw:(g + 1) * gw])
        st_sc[:, g * gw:(g + 1) * gw] = blkdec_e[:, g * gw:(g + 1) * gw] * st_g + ds
        ys_parts.append(_dot(cm_g, st_g.astype(BF16)))
        for hp in range(gw // LANES):
            c0 = g * gw + hp * LANES
            rhs = dtx_b[:, c0:c0 + LANES]
            pair = None
            for half in range(2):
                h = c0 // B_HEAD_DIM + half
                seg = acs[:, h:h + 1] - acs_t[h:h + 1, :]
                decay = jnp.exp(jnp.where(col <= row, seg, -jnp.inf))
                m_h = (cb * decay).astype(BF16)
                keep = (lane_hp < B_HEAD_DIM) if half == 0 else (lane_hp >= B_HEAD_DIM)
                part = _dot(m_h, jnp.where(keep, rhs, jnp.zeros_like(rhs)))
                pair = part if pair is None else pair + part
            yi_parts.append(pair)
    y = (jnp.concatenate(yi_parts, axis=1) + jnp.concatenate(ys_parts, axis=1) * expacs_e
         + dskip_ref[...] * xs)
    y = y * _silu(zb_ref[...])
    half_w = B_WIDTH // B_GROUPS
    yb = jnp.concatenate(
        [_rms(y[:, g * half_w:(g + 1) * half_w], gssd_ref[:, g * half_w:(g + 1) * half_w])
         for g in range(B_GROUPS)], axis=1)
    y_ref[...] = jnp.concatenate([ya, yb], axis=1).astype(y_ref.dtype)

    @pl.when(j == last)
    def _():
        sfin_ref[...] = st_sc[...].T


def even_mixer(u, v, za, zb, xbc, dt, ws, bs_t, gv, cw, cb, dtb, alog, dskip_e, gssd, e_mat, conv0, s0, blk):
    b, l, _ = u.shape
    nblk = l // blk
    tok = lambda w: pl.BlockSpec((None, blk, w), lambda i, j: (i, j, 0))
    per_b = lambda s: pl.BlockSpec((None,) + s, lambda i, j: (i, 0, 0))
    consts = [ws, bs_t, gv, cw, cb, dtb, alog, dskip_e, gssd, e_mat]
    hp = B_HEADS * B_HEAD_DIM
    return pl.pallas_call(
        functools.partial(_even_mixer_body, blk=blk),
        grid=(b, nblk),
        in_specs=[tok(1024), tok(1024), tok(1024), tok(1024), tok(B_CONV_DIM), tok(LANES)]
        + [_const_spec(c.shape) for c in consts]
        + [per_b((B_CONV - 1, B_CONV_DIM)), per_b((hp, B_STATE))],
        out_specs=[tok(2048), tok(1024), per_b((hp, B_STATE)), per_b((B_CONV - 1, B_CONV_DIM))],
        out_shape=[jax.ShapeDtypeStruct((b, l, 2048), BF16), jax.ShapeDtypeStruct((b, l, 1024), F32),
                   jax.ShapeDtypeStruct((b, hp, B_STATE), F32),
                   jax.ShapeDtypeStruct((b, B_CONV - 1, B_CONV_DIM), F32)],
        scratch_shapes=[pltpu.VMEM((CTX_ROW + blk, B_CONV_DIM), F32), pltpu.VMEM((B_STATE, hp), F32)],
        compiler_params=_params(("parallel", "arbitrary")),
        name="even_mixer",
    )(u, v, za, zb, xbc, dt, *consts, conv0, s0)


def _fox_proj_body(x_ref, g_ref, wq_ref, wk_ref, wv_ref, wz_ref, wf_ref, gq_ref, gk_ref, bf_ref,
                   q_ref, kf_ref, kb_ref, vf_ref, vb_ref, z_ref, lf_ref):
    h = _rms(x_ref[...], g_ref[...]).astype(BF16)
    q = _dot(h, wq_ref[...])
    k = _dot(h, wk_ref[...])
    scale = C_HEAD_DIM ** -0.5
    for hd in range(C_HEADS):
        sl = slice(hd * C_HEAD_DIM, (hd + 1) * C_HEAD_DIM)
        q_ref[:, sl] = (_rms(q[:, sl], gq_ref[...]) * scale).astype(BF16)
        kn = _rms(k[:, sl], gk_ref[...])
        kf_ref[:, sl] = kn
        kb_ref[:, sl] = kn.astype(BF16)
    v = _dot(h, wv_ref[...])
    vf_ref[...] = v
    vb_ref[...] = v.astype(BF16)
    z_ref[...] = _dot(h, wz_ref[...])
    lf_ref[...] = jax.nn.log_sigmoid(_dot(h, wf_ref[...]) + bf_ref[...])


def fox_proj(x, g, wq, wk, wv, wz, wf, gq, gk, bf, tm):
    t, d = x.shape
    tok = lambda w: pl.BlockSpec((tm, w), lambda i: (i, 0))
    ins = [g, wq, wk, wv, wz, wf, gq, gk, bf]
    return pl.pallas_call(
        _fox_proj_body,
        grid=(t // tm,),
        in_specs=[tok(d)] + [_const_spec(a.shape) for a in ins],
        out_specs=[tok(d)] * 6 + [tok(LANES)],
        out_shape=[jax.ShapeDtypeStruct((t, d), BF16), jax.ShapeDtypeStruct((t, d), F32),
                   jax.ShapeDtypeStruct((t, d), BF16), jax.ShapeDtypeStruct((t, d), F32),
                   jax.ShapeDtypeStruct((t, d), BF16), jax.ShapeDtypeStruct((t, d), F32),
                   jax.ShapeDtypeStruct((t, LANES), F32)],
        compiler_params=_params(("parallel",)),
        name="fox_proj",
    )(x, *ins)


def _fox_aug_body(lf_ref, pq_ref, pk_ref, oq_ref, ok_ref, qa_ref, ka_ref, carry_sc, *, ts):
    @pl.when(pl.program_id(1) == 0)
    def _():
        carry_sc[...] = jnp.zeros_like(carry_sc)

    row = lax.broadcasted_iota(jnp.int32, (ts, ts), 0)
    col = lax.broadcasted_iota(jnp.int32, (ts, ts), 1)
    fc = _dot3_rhs((col <= row).astype(BF16), lf_ref[...]) + carry_sc[...]
    carry_sc[...] = fc[ts - 1:ts, :]
    cat = jnp.concatenate(_split3(fc), axis=1)
    qa_ref[...] = (_dot(cat, pq_ref[...]) + oq_ref[...]).astype(BF16)
    ka_ref[...] = (ok_ref[...] - _dot(cat, pk_ref[...])).astype(BF16)


def _aug_constants():
    w = C_HEADS * C_HEAD_DIM
    pq = np.zeros((3 * LANES, w), np.float32)
    pk = np.zeros((3 * LANES, w), np.float32)
    oq = np.zeros((1, w), np.float32)
    ok = np.zeros((1, w), np.float32)
    for h in range(C_HEADS):
        for j in range(3):
            pq[j * LANES + h, h * C_HEAD_DIM + j] = 1.0
            pk[j * LANES + h, h * C_HEAD_DIM + 3 + j] = 1.0
            oq[0, h * C_HEAD_DIM + 3 + j] = 1.0
            ok[0, h * C_HEAD_DIM + j] = 1.0
    return jnp.asarray(pq, BF16), jnp.asarray(pk, BF16), jnp.asarray(oq), jnp.asarray(ok)


def fox_aug(lf, ts):
    b, s, _ = lf.shape
    w = C_HEADS * C_HEAD_DIM
    pq, pk, oq, ok = _aug_constants()
    return pl.pallas_call(
        functools.partial(_fox_aug_body, ts=ts),
        grid=(b, s // ts),
        in_specs=[pl.BlockSpec((None, ts, LANES), lambda i, j: (i, j, 0)),
                  _const_spec(pq.shape), _const_spec(pk.shape), _const_spec(oq.shape), _const_spec(ok.shape)],
        out_specs=[pl.BlockSpec((None, ts, w), lambda i, j: (i, j, 0))] * 2,
        out_shape=[jax.ShapeDtypeStruct((b, s, w), BF16)] * 2,
        scratch_shapes=[pltpu.VMEM((1, LANES), F32)],
        compiler_params=_params(("parallel", "arbitrary")),
        name="fox_aug",
    )(lf, pq, pk, oq, ok)


def _fox_attn_body(q_ref, qa_ref, k_ref, ka_ref, v_ref, z_ref, o_ref, *, tq):
    qi = pl.program_id(2)
    qc = jnp.concatenate([q_ref[...], qa_ref[...]], axis=1)

    def tile(j, carry, masked):
        m, l, acc = carry
        r0 = pl.multiple_of(j * tq, tq)
        kc = jnp.concatenate([k_ref[pl.ds(r0, tq), :], ka_ref[pl.ds(r0, tq), :]], axis=1)
        s = _dot_nt(qc, kc)
        if masked:
            row = lax.broadcasted_iota(jnp.int32, (tq, tq), 0)
            col = lax.broadcasted_iota(jnp.int32, (tq, tq), 1)
            s = jnp.where(col <= row, s, -jnp.inf)
        m_new = jnp.maximum(m, jnp.max(s, axis=-1, keepdims=True))
        alpha = jnp.exp(m - m_new)
        p = jnp.exp(s - m_new)
        l = alpha * l + jnp.sum(p, axis=-1, keepdims=True)
        acc = alpha * acc + _dot(p.astype(BF16), v_ref[pl.ds(r0, tq), :])
        return m_new, l, acc

    init = (jnp.full((tq, 1), -jnp.inf, F32), jnp.zeros((tq, 1), F32), jnp.zeros((tq, C_HEAD_DIM), F32))
    carry = lax.fori_loop(0, qi, lambda j, c: tile(j, c, False), init)
    _, l, acc = tile(qi, carry, True)
    o_ref[...] = (_silu(z_ref[...]) * (acc / l)).astype(o_ref.dtype)


def fox_attn(q, qa, k, ka, v, z, tq):
    b, l, w = q.shape
    qspec = pl.BlockSpec((None, tq, C_HEAD_DIM), lambda i, h, j: (i, j, h))
    kspec = pl.BlockSpec((None, l, C_HEAD_DIM), lambda i, h, j: (i, 0, h))
    return pl.pallas_call(
        functools.partial(_fox_attn_body, tq=tq),
        grid=(b, C_HEADS, l // tq),
        in_specs=[qspec, qspec, kspec, kspec, kspec, qspec],
        out_specs=qspec,
        out_shape=jax.ShapeDtypeStruct((b, l, w), BF16),
        compiler_params=_params(("parallel", "parallel", "arbitrary")),
        name="fox_attn",
    )(q, qa, k, ka, v, z)


def _fox_attn_sample_body(q_ref, qa_ref, kp_ref, kap_ref, vp_ref, kn_ref, kan_ref, vn_ref, z_ref, o_ref):
    n = q_ref.shape[0]
    qc = jnp.concatenate([q_ref[...], qa_ref[...]], axis=1)
    s_past = _dot_nt(qc, jnp.concatenate([kp_ref[...].astype(BF16), kap_ref[...]], axis=1))
    s_new = _dot_nt(qc, jnp.concatenate([kn_ref[...], kan_ref[...]], axis=1))
    row = lax.broadcasted_iota(jnp.int32, (n, n), 0)
    col = lax.broadcasted_iota(jnp.int32, (n, n), 1)
    s_new = jnp.where(col <= row, s_new, -jnp.inf)
    m = jnp.maximum(jnp.max(s_past, axis=-1, keepdims=True), jnp.max(s_new, axis=-1, keepdims=True))
    p_past = jnp.exp(s_past - m)
    p_new = jnp.exp(s_new - m)
    l = jnp.sum(p_past, axis=-1, keepdims=True) + jnp.sum(p_new, axis=-1, keepdims=True)
    acc = _dot(p_past.astype(BF16), vp_ref[...].astype(BF16)) + _dot(p_new.astype(BF16), vn_ref[...])
    o_ref[...] = (_silu(z_ref[...]) * (acc / l)).astype(o_ref.dtype)


def fox_attn_sample(q, qa_all, k_past, ka_all, v_past, k_new, v_new, z):
    b, n, w = q.shape
    p0 = k_past.shape[1]
    new = pl.BlockSpec((None, n, C_HEAD_DIM), lambda i, h: (i, 0, h))
    new_of_all = pl.BlockSpec((None, n, C_HEAD_DIM), lambda i, h: (i, p0 // n, h))
    past = pl.BlockSpec((None, p0, C_HEAD_DIM), lambda i, h: (i, 0, h))
    return pl.pallas_call(
        _fox_attn_sample_body,
        grid=(b, C_HEADS),
        in_specs=[new, new_of_all, past, past, past, new, new_of_all, new, new],
        out_specs=new,
        out_shape=jax.ShapeDtypeStruct((b, n, w), BF16),
        compiler_params=_params(("parallel", "parallel")),
        name="fox_attn_sample",
    )(q, qa_all, k_past, ka_all, v_past, k_new, ka_all, v_new, z)


def _largest_tile(n, cap, mult):
    t = min(n, cap)
    while n % t or t % mult:
        t -= 1
    return t


def _pad_cols(a, width):
    return jnp.pad(a, ((0, 0), (0, width - a.shape[1])))


def _even_weights(w_in, w_out, ws, bs, gv, conv_w, conv_b, dt_bias, a_log, d_skip, g_ssd):
    a_w = A_GROUPS * A_GROUP_DIM
    cuts = (a_w, 2 * a_w, 3 * a_w, 3 * a_w + B_WIDTH, 3 * a_w + B_WIDTH + B_CONV_DIM)
    pieces = jnp.split(w_in, cuts, axis=1)
    pieces[-1] = _pad_cols(pieces[-1], LANES)
    e_mat = np.zeros((LANES, B_HEADS * B_HEAD_DIM), np.float32)
    for h in range(B_HEADS):
        e_mat[h, h * B_HEAD_DIM:(h + 1) * B_HEAD_DIM] = 1.0
    return dict(
        w_in=[p.astype(BF16) for p in pieces], w_out=w_out.astype(BF16), ws=ws, bs=bs, gv=gv,
        cw=conv_w.T, cb=conv_b[None, :], dtb=_pad_cols(dt_bias[None, :], LANES),
        alog=_pad_cols(a_log[None, :], LANES), dskip_e=jnp.repeat(d_skip, B_HEAD_DIM)[None, :],
        gssd=g_ssd[None, :], e_mat=jnp.asarray(e_mat, BF16))


def _even_layer(x, g_pre, g_post, wts, conv0, s0):
    b, l, d = x.shape
    t = b * l
    x2 = x.reshape(t, d)
    u, v, za, zb, xbc, dt = norm_proj(x2, g_pre[None, :], wts["w_in"], _largest_tile(t, 256, 8))
    blk = min(l, A_CHUNK)
    r3 = lambda a: a.reshape(b, l, a.shape[1])
    ycat, vrows, sfin, cfin = even_mixer(
        r3(u), r3(v), r3(za), r3(zb), r3(xbc), r3(dt),
        wts["ws"][:, :blk, :blk], wts["bs"][:, :blk].T, wts["gv"], wts["cw"], wts["cb"], wts["dtb"],
        wts["alog"], wts["dskip_e"], wts["gssd"], wts["e_mat"],
        conv0, s0.reshape(b, B_HEADS * B_HEAD_DIM, B_STATE), blk)
    xn = out_proj(ycat.reshape(t, ycat.shape[2]), wts["w_out"], x2, g_post[None, :], _largest_tile(t, 512, 8))
    return (xn.reshape(b, l, d), vrows, sfin.reshape(b, B_HEADS, B_HEAD_DIM, B_STATE), cfin)


def _odd_weights(w_in, b_f, w_out, gq, gk):
    w = C_HEADS * C_HEAD_DIM
    wq, wk, wv, wz, wf = jnp.split(w_in, (w, 2 * w, 3 * w, 4 * w), axis=1)
    return dict(wq=wq.astype(BF16), wk=wk.astype(BF16), wv=wv.astype(BF16), wz=wz.astype(BF16),
                wf=_pad_cols(wf, LANES).astype(BF16), bf=_pad_cols(b_f[None, :], LANES),
                w_out=w_out.astype(BF16), gq=gq[None, :], gk=gk[None, :])


def _odd_layer(x, g_pre, g_post, wts, past_k, past_v, past_logf):
    b, l, d = x.shape
    t = b * l
    x2 = x.reshape(t, d)
    q, kf, kb, vf, vb, z, lf = fox_proj(x2, g_pre[None, :], wts["wq"], wts["wk"], wts["wv"], wts["wz"],
                                        wts["wf"], wts["gq"], wts["gk"], wts["bf"], _largest_tile(t, 256, 16))
    r3 = lambda a: a.reshape(b, l, a.shape[1])
    lf3 = r3(lf)
    if past_k is None:
        qa, ka = fox_aug(lf3, _largest_tile(l, 512, 16))
        yc = fox_attn(r3(q), qa, r3(kb), ka, r3(vb), r3(z), _largest_tile(l, 512, 16))
    else:
        p0 = past_k.shape[1]
        lf_all = jnp.concatenate([_pad_cols(past_logf.reshape(b * p0, C_HEADS), LANES).reshape(b, p0, LANES), lf3],
                                 axis=1)
        qa, ka = fox_aug(lf_all, _largest_tile(p0 + l, 512, 16))
        yc = fox_attn_sample(r3(q), qa, past_k.reshape(b, p0, d), ka, past_v.reshape(b, p0, d),
                             r3(kb), r3(vb), r3(z))
    xn = out_proj(yc.reshape(t, d), wts["w_out"], x2, g_post[None, :], _largest_tile(t, 512, 8))
    hd = (b, l, C_HEADS, C_HEAD_DIM)
    return xn.reshape(b, l, d), kf.reshape(hd), vf.reshape(hd), lf3[:, :, :C_HEADS]


def kernel(x_prompt, x_sample, cache_fox_k, cache_fox_v, cache_fox_logf, state_ssd, state_conv, norm_pre, norm_post, w_in_even, w_out_even, gmlp_ws, gmlp_bs, gmlp_gv, ssd_conv_w, ssd_conv_b, ssd_dt_bias, ssd_a_log, ssd_d, ssd_norm_g, w_in_odd, fox_b_forget, w_out_odd, fox_gq, fox_gk):
    xp, xs = x_prompt, x_sample
    bp = xp.shape[0]
    depth = norm_pre.shape[0]
    outs = {n: [] for n in ("kp", "vp", "lp", "sp", "cp", "ks", "vs", "ls", "ss", "cs", "gs")}
    for i in range(depth):
        j = i // 2
        if i % 2 == 0:
            wts = _even_weights(w_in_even[j], w_out_even[j], gmlp_ws[j], gmlp_bs[j], gmlp_gv[j], ssd_conv_w[j],
                                ssd_conv_b[j], ssd_dt_bias[j], ssd_a_log[j], ssd_d[j], ssd_norm_g[j])
            xp, _, sp, cp = _even_layer(xp, norm_pre[i], norm_post[i], wts,
                                        jnp.zeros((bp, B_CONV - 1, B_CONV_DIM), F32),
                                        jnp.zeros((bp, B_HEADS, B_HEAD_DIM, B_STATE), F32))
            xs, gs, ss, cs = _even_layer(xs, norm_pre[i], norm_post[i], wts, state_conv[j], state_ssd[j])
            for n, a in (("sp", sp), ("cp", cp), ("ss", ss), ("cs", cs), ("gs", gs)):
                outs[n].append(a)
        else:
            wts = _odd_weights(w_in_odd[j], fox_b_forget[j], w_out_odd[j], fox_gq[j], fox_gk[j])
            xp, kp, vp, lp = _odd_layer(xp, norm_pre[i], norm_post[i], wts, None, None, None)
            xs, ks, vs, ls = _odd_layer(xs, norm_pre[i], norm_post[i], wts,
                                        cache_fox_k[j], cache_fox_v[j], cache_fox_logf[j])
            for n, a in (("kp", kp), ("vp", vp), ("lp", lp), ("ks", ks), ("vs", vs), ("ls", ls)):
                outs[n].append(a)
    st = {n: jnp.stack(a) for n, a in outs.items()}
    return (xp, xs, st["kp"], st["vp"], st["lp"], st["sp"], st["cp"],
            st["ks"], st["vs"], st["ls"], st["ss"], st["cs"], st["gs"])
```

```python
import functools

import numpy as np
import jax
import jax.numpy as jnp
from jax import lax
from jax.experimental import pallas as pl
from jax.experimental.pallas import tpu as pltpu

F32 = jnp.float32
BF16 = jnp.bfloat16
EPS = 1e-6
LANES = 128
VMEM_LIMIT = 56 * 1024 * 1024

D_MODEL = 1024
A_GROUPS = 4
A_GROUP_DIM = 256
A_CHUNK = 128
SUB_CHUNK = 64
B_HEADS = 16
B_HEAD_DIM = 64
B_GROUPS = 2
B_STATE = 128
B_CONV = 4
B_WIDTH = 1024
B_CONV_DIM = B_WIDTH + 2 * B_GROUPS * B_STATE
CTX_ROW = 8
C_HEADS = 8
C_HEAD_DIM = 128
SQRT_HALF = 0.7071067811865476
LOG2E = 1.4426950408889634


def _params(sem):
    return pltpu.CompilerParams(dimension_semantics=sem, vmem_limit_bytes=VMEM_LIMIT)


def _const_spec(shape):
    nd = len(shape)
    return pl.BlockSpec(shape, lambda *_: (0,) * nd)


def _dot(a, b):
    return jnp.dot(a, b, preferred_element_type=F32)


def _dot_nt(a, b):
    return lax.dot_general(a, b, (((1,), (1,)), ((), ())), preferred_element_type=F32)


def _split3(x):
    hi = x.astype(BF16)
    r = x - hi.astype(F32)
    mid = r.astype(BF16)
    lo = (r - mid.astype(F32)).astype(BF16)
    return hi, mid, lo


def _dot3_lhs(x, w):
    hi, mid, lo = _split3(x)
    return _dot(hi, w) + _dot(mid, w) + _dot(lo, w)


def _dot3_rhs(w, x):
    hi, mid, lo = _split3(x)
    return _dot(w, hi) + _dot(w, mid) + _dot(w, lo)


def _rms(x, g):
    return x * lax.rsqrt(jnp.mean(x * x, axis=-1, keepdims=True) + EPS) * g


def _gelu(x):
    return 0.5 * x * (1.0 + lax.erf(x * SQRT_HALF))


def _silu(x):
    return x * jax.nn.sigmoid(x)


def _norm_proj_body(x_ref, g_ref, *refs):
    n = len(refs) // 2
    h = _rms(x_ref[...], g_ref[...]).astype(BF16)
    for w_ref, o_ref in zip(refs[:n], refs[n:]):
        o_ref[...] = _dot(h, w_ref[...])


def norm_proj(x, g, ws, tm):
    t, d = x.shape
    return pl.pallas_call(
        _norm_proj_body,
        grid=(t // tm,),
        in_specs=[pl.BlockSpec((tm, d), lambda i: (i, 0)), _const_spec((1, d))]
        + [_const_spec(w.shape) for w in ws],
        out_specs=[pl.BlockSpec((tm, w.shape[1]), lambda i: (i, 0)) for w in ws],
        out_shape=[jax.ShapeDtypeStruct((t, w.shape[1]), F32) for w in ws],
        compiler_params=_params(("parallel",)),
        name="norm_proj",
    )(x, g, *ws)


def _out_proj_body(y_ref, w_ref, x_ref, g_ref, o_ref):
    o = _dot(y_ref[...], w_ref[...])
    o_ref[...] = x_ref[...] + _rms(o, g_ref[...])


def out_proj(y, w, x, g, tm):
    t, d = x.shape
    kd = y.shape[1]
    return pl.pallas_call(
        _out_proj_body,
        grid=(t // tm,),
        in_specs=[pl.BlockSpec((tm, kd), lambda i: (i, 0)), _const_spec(w.shape),
                  pl.BlockSpec((tm, d), lambda i: (i, 0)), _const_spec((1, d))],
        out_specs=pl.BlockSpec((tm, d), lambda i: (i, 0)),
        out_shape=jax.ShapeDtypeStruct((t, d), F32),
        compiler_params=_params(("parallel",)),
        name="out_proj",
    )(y, w, x, g)


def _even_mixer_body(u_ref, v_ref, za_ref, zb_ref, xbc_ref, dt_ref,
                     ws_ref, bs_ref, gv_ref, cw_ref, cb_ref, dtb_ref, alog_ref, dskip_ref, gssd_ref,
                     e_ref, conv0_ref, s0_ref, *refs, blk, emit_v):
    y_ref, sfin_ref, cfin_ref, xpad_sc, st_sc = refs[:1] + refs[-4:]
    j = pl.program_id(1)
    last = pl.num_programs(1) - 1
    row = lax.broadcasted_iota(jnp.int32, (blk, blk), 0)
    col = lax.broadcasted_iota(jnp.int32, (blk, blk), 1)

    gu = _gelu(u_ref[...])
    gv = _gelu(v_ref[...])
    vn = jnp.concatenate(
        [_rms(gv[:, g * A_GROUP_DIM:(g + 1) * A_GROUP_DIM], gv_ref[g:g + 1, :]) for g in range(A_GROUPS)],
        axis=1)
    if emit_v:
        refs[1][...] = vn
    chunk_causal = (col // SUB_CHUNK) <= (row // SUB_CHUNK)
    s_parts = []
    for g in range(A_GROUPS):
        wg = jnp.where(chunk_causal, ws_ref[g], 0.0).astype(BF16)
        sg = _dot(wg, vn[:, g * A_GROUP_DIM:(g + 1) * A_GROUP_DIM].astype(BF16))
        s_parts.append(sg + bs_ref[:, g:g + 1])
    ya = _silu(za_ref[...]) * gu * jnp.concatenate(s_parts, axis=1)

    @pl.when(j == 0)
    def _():
        xpad_sc[CTX_ROW - (B_CONV - 1):CTX_ROW, :] = conv0_ref[...]
        st_sc[...] = s0_ref[...].T

    xpad_sc[CTX_ROW:CTX_ROW + blk, :] = xbc_ref[...]
    acc = cb_ref[...]
    for k in range(B_CONV):
        r0 = CTX_ROW - (B_CONV - 1) + k
        acc = acc + xpad_sc[r0:r0 + blk, :] * cw_ref[k:k + 1, :]
    xc = _silu(acc)
    new_ctx = xpad_sc[CTX_ROW + blk - (B_CONV - 1):CTX_ROW + blk, :]
    xpad_sc[CTX_ROW - (B_CONV - 1):CTX_ROW, :] = new_ctx

    @pl.when(j == last)
    def _():
        cfin_ref[...] = new_ctx

    xs = xc[:, :B_WIDTH]
    bm = xc[:, B_WIDTH:B_WIDTH + B_GROUPS * B_STATE]
    cm = xc[:, B_WIDTH + B_GROUPS * B_STATE:]

    lane = lax.broadcasted_iota(jnp.int32, (1, LANES), 1)
    dt = jax.nn.softplus(dt_ref[...] + dtb_ref[...])
    a = jnp.where(lane < B_HEADS, -jnp.exp(alog_ref[...]), 0.0)
    tril = (col <= row).astype(BF16)
    acs = _dot3_rhs(tril, dt * a)
    acs_t = acs.T
    acs_last = acs[blk - 1:blk, :]
    e = e_ref[...]
    dt_e = _dot3_lhs(dt, e)
    expacs_e = _dot3_lhs(jnp.exp(acs), e)
    toend_e = _dot3_lhs(jnp.exp(acs_last - acs), e)
    blkdec_e = expacs_e[blk - 1:blk, :]
    dtx = dt_e * xs
    wds = (toend_e * dtx).astype(BF16)
    dtx_b = dtx.astype(BF16)
    lane_hp = lax.broadcasted_iota(jnp.int32, (blk, LANES), 1)
    gw = (B_HEADS // B_GROUPS) * B_HEAD_DIM
    ys_parts, yi_parts = [], []
    for g in range(B_GROUPS):
        bm_g = bm[:, g * B_STATE:(g + 1) * B_STATE]
        cm_g = cm[:, g * B_STATE:(g + 1) * B_STATE].astype(BF16)
        cb = _dot_nt(cm_g, bm_g.astype(BF16))
        st_g = st_sc[:, g * gw:(g + 1) * gw]
        ds = _dot(bm_g.T.astype(BF16), wds[:, g * gw:(g + 1) * gw])
        st_sc[:, g * gw:(g + 1) * gw] = blkdec_e[:, g * gw:(g + 1) * gw] * st_g + ds
        ys_parts.append(_dot(cm_g, st_g.astype(BF16)))
        for hp in range(gw // LANES):
            c0 = g * gw + hp * LANES
            rhs = dtx_b[:, c0:c0 + LANES]
            pair = None
            for half in range(2):
                h = c0 // B_HEAD_DIM + half
                seg = acs[:, h:h + 1] - acs_t[h:h + 1, :]
                decay = jnp.exp(jnp.where(col <= row, seg, -jnp.inf))
                m_h = (cb * decay).astype(BF16)
                keep = (lane_hp < B_HEAD_DIM) if half == 0 else (lane_hp >= B_HEAD_DIM)
                part = _dot(m_h, jnp.where(keep, rhs, jnp.zeros_like(rhs)))
                pair = part if pair is None else pair + part
            yi_parts.append(pair)
    y = (jnp.concatenate(yi_parts, axis=1) + jnp.concatenate(ys_parts, axis=1) * expacs_e
         + dskip_ref[...] * xs)
    y = y * _silu(zb_ref[...])
    half_w = B_WIDTH // B_GROUPS
    yb = jnp.concatenate(
        [_rms(y[:, g * half_w:(g + 1) * half_w], gssd_ref[:, g * half_w:(g + 1) * half_w])
         for g in range(B_GROUPS)], axis=1)
    y_ref[...] = jnp.concatenate([ya, yb], axis=1).astype(y_ref.dtype)

    @pl.when(j == last)
    def _():
        sfin_ref[...] = st_sc[...].T


def even_mixer(u, v, za, zb, xbc, dt, ws, bs_t, gv, cw, cb, dtb, alog, dskip_e, gssd, e_mat, conv0, s0, blk, emit_v):
    b, l, _ = u.shape
    nblk = l // blk
    tok = lambda w: pl.BlockSpec((None, blk, w), lambda i, j: (i, j, 0))
    per_b = lambda s: pl.BlockSpec((None,) + s, lambda i, j: (i, 0, 0))
    consts = [ws, bs_t, gv, cw, cb, dtb, alog, dskip_e, gssd, e_mat]
    hp = B_HEADS * B_HEAD_DIM
    v_spec = [tok(1024)] if emit_v else []
    v_shape = [jax.ShapeDtypeStruct((b, l, 1024), F32)] if emit_v else []
    return pl.pallas_call(
        functools.partial(_even_mixer_body, blk=blk, emit_v=emit_v),
        grid=(b, nblk),
        in_specs=[tok(1024), tok(1024), tok(1024), tok(1024), tok(B_CONV_DIM), tok(LANES)]
        + [_const_spec(c.shape) for c in consts]
        + [per_b((B_CONV - 1, B_CONV_DIM)), per_b((hp, B_STATE))],
        out_specs=[tok(2048)] + v_spec + [per_b((hp, B_STATE)), per_b((B_CONV - 1, B_CONV_DIM))],
        out_shape=[jax.ShapeDtypeStruct((b, l, 2048), BF16)] + v_shape
        + [jax.ShapeDtypeStruct((b, hp, B_STATE), F32), jax.ShapeDtypeStruct((b, B_CONV - 1, B_CONV_DIM), F32)],
        scratch_shapes=[pltpu.VMEM((CTX_ROW + blk, B_CONV_DIM), F32), pltpu.VMEM((B_STATE, hp), F32)],
        compiler_params=_params(("parallel", "arbitrary")),
        name="even_mixer",
    )(u, v, za, zb, xbc, dt, *consts, conv0, s0)


def _fox_proj_body(x_ref, g_ref, wq_ref, wk_ref, wv_ref, wz_ref, wf_ref, gq_ref, gk_ref, bf_ref, *refs,
                   slot, first):
    q_ref, kf_ref, kb_ref, vf_ref, vb_ref, z_ref, lf_ref = refs[-7:]
    if first:
        kf_ref[...] = jnp.zeros(kf_ref.shape, F32)
        vf_ref[...] = jnp.zeros(vf_ref.shape, F32)
        kf_ref, vf_ref = kf_ref.at[slot], vf_ref.at[slot]
    h = _rms(x_ref[...], g_ref[...]).astype(BF16)
    q = _dot(h, wq_ref[...])
    k = _dot(h, wk_ref[...])
    v = _dot(h, wv_ref[...])
    q_scale = C_HEAD_DIM ** -0.5 * LOG2E
    for hd in range(C_HEADS):
        sl = slice(hd * C_HEAD_DIM, (hd + 1) * C_HEAD_DIM)
        q_ref[:, sl] = (_rms(q[:, sl], gq_ref[...]) * q_scale).astype(BF16)
        kn = _rms(k[:, sl], gk_ref[...])
        kf_ref[:, hd, :] = kn
        kb_ref[:, sl] = kn.astype(BF16)
        vf_ref[:, hd, :] = v[:, sl]
    vb_ref[...] = v.astype(BF16)
    z_ref[...] = _dot(h, wz_ref[...])
    lf_ref[...] = jax.nn.log_sigmoid(_dot(h, wf_ref[...]) + bf_ref[...])


def fox_proj(x, g, wq, wk, wv, wz, wf, gq, gk, bf, tm, n_slots, slot, prev_k, prev_v):
    t, d = x.shape
    tok = lambda w: pl.BlockSpec((tm, w), lambda i: (i, 0))
    ins = [g, wq, wk, wv, wz, wf, gq, gk, bf]
    first = prev_k is None
    stacked = jax.ShapeDtypeStruct((n_slots, t, C_HEADS, C_HEAD_DIM), F32)
    if first:
        st_spec = pl.BlockSpec((n_slots, tm, C_HEADS, C_HEAD_DIM), lambda i: (0, i, 0, 0))
        extra, extra_specs, aliases = [], [], {}
    else:
        st_spec = pl.BlockSpec((None, tm, C_HEADS, C_HEAD_DIM), lambda i: (slot, i, 0, 0))
        extra, extra_specs = [prev_k, prev_v], [pl.BlockSpec(memory_space=pl.ANY)] * 2
        aliases = {1 + len(ins): 1, 2 + len(ins): 3}
    return pl.pallas_call(
        functools.partial(_fox_proj_body, slot=slot, first=first),
        grid=(t // tm,),
        in_specs=[tok(d)] + [_const_spec(a.shape) for a in ins] + extra_specs,
        out_specs=[tok(d), st_spec, tok(d), st_spec, tok(d), tok(d), tok(LANES)],
        out_shape=[jax.ShapeDtypeStruct((t, d), BF16), stacked, jax.ShapeDtypeStruct((t, d), BF16), stacked,
                   jax.ShapeDtypeStruct((t, d), BF16), jax.ShapeDtypeStruct((t, d), F32),
                   jax.ShapeDtypeStruct((t, LANES), F32)],
        input_output_aliases=aliases,
        compiler_params=_params(("parallel",)),
        name="fox_proj",
    )(x, *ins, *extra)


def _fox_aug_body(lf_ref, pq_ref, pk_ref, oq_ref, ok_ref, qa_ref, ka_ref, carry_sc, *, ts):
    @pl.when(pl.program_id(1) == 0)
    def _():
        carry_sc[...] = jnp.zeros_like(carry_sc)

    row = lax.broadcasted_iota(jnp.int32, (ts, ts), 0)
    col = lax.broadcasted_iota(jnp.int32, (ts, ts), 1)
    lane = lax.broadcasted_iota(jnp.int32, (1, LANES), 1)
    lf = jnp.where(lane < C_HEADS, lf_ref[...], 0.0)
    fc = _dot3_rhs((col <= row).astype(BF16), lf) + carry_sc[...]
    carry_sc[...] = fc[ts - 1:ts, :]
    hi, mid, lo = _split3(fc * LOG2E)
    pieces = (hi.astype(F32) + pltpu.roll(mid.astype(F32), C_HEADS, axis=1)
              + pltpu.roll(lo.astype(F32), 2 * C_HEADS, axis=1)).astype(BF16)
    qa_ref[...] = (_dot(pieces, pq_ref[...]) + oq_ref[...]).astype(BF16)
    ka_ref[...] = (ok_ref[...] - _dot(pieces, pk_ref[...])).astype(BF16)


def _aug_constants():
    w = C_HEADS * C_HEAD_DIM
    pq = np.zeros((LANES, w), np.float32)
    pk = np.zeros((LANES, w), np.float32)
    oq = np.zeros((1, w), np.float32)
    ok = np.zeros((1, w), np.float32)
    for h in range(C_HEADS):
        for j in range(3):
            pq[j * C_HEADS + h, h * C_HEAD_DIM + j] = 1.0
            pk[j * C_HEADS + h, h * C_HEAD_DIM + 3 + j] = 1.0
            oq[0, h * C_HEAD_DIM + 3 + j] = 1.0
            ok[0, h * C_HEAD_DIM + j] = 1.0
    return jnp.asarray(pq, BF16), jnp.asarray(pk, BF16), jnp.asarray(oq), jnp.asarray(ok)


def fox_aug(lf, ts):
    b, s, _ = lf.shape
    w = C_HEADS * C_HEAD_DIM
    pq, pk, oq, ok = _aug_constants()
    return pl.pallas_call(
        functools.partial(_fox_aug_body, ts=ts),
        grid=(b, s // ts),
        in_specs=[pl.BlockSpec((None, ts, LANES), lambda i, j: (i, j, 0)),
                  _const_spec(pq.shape), _const_spec(pk.shape), _const_spec(oq.shape), _const_spec(ok.shape)],
        out_specs=[pl.BlockSpec((None, ts, w), lambda i, j: (i, j, 0))] * 2,
        out_shape=[jax.ShapeDtypeStruct((b, s, w), BF16)] * 2,
        scratch_shapes=[pltpu.VMEM((1, LANES), F32)],
        compiler_params=_params(("parallel", "arbitrary")),
        name="fox_aug",
    )(lf, pq, pk, oq, ok)


ATTN_ROWS = 512
ATTN_CHAINS = 2


def _softmax_step(qc, carry, kc, vb, diagonal):
    m, l, acc = carry
    s = _dot_nt(qc, kc)
    if diagonal:
        row = lax.broadcasted_iota(jnp.int32, s.shape, 0)
        col = lax.broadcasted_iota(jnp.int32, s.shape, 1)
        s = jnp.where(col <= row, s, -jnp.inf)
    m_new = jnp.maximum(m, jnp.max(s, axis=-1, keepdims=True))
    alpha = jnp.exp2(m - m_new)
    p = jnp.exp2(s - m_new)
    l = alpha * l + jnp.sum(p, axis=-1, keepdims=True)
    acc = alpha * acc + _dot(p.astype(BF16), vb)
    return m_new, l, acc


def _fox_attn_body(q_ref, qa_ref, k_ref, ka_ref, v_ref, z_ref, o_ref, *, rows, chains):
    qi = pl.program_id(2)
    tq = rows * chains
    qcs = [jnp.concatenate([q_ref[c * rows:(c + 1) * rows, :], qa_ref[c * rows:(c + 1) * rows, :]], axis=1)
           for c in range(chains)]

    def kv(r0, n):
        return (jnp.concatenate([k_ref[pl.ds(r0, n), :], ka_ref[pl.ds(r0, n), :]], axis=1), v_ref[pl.ds(r0, n), :])

    def below_diagonal(j, carries):
        kc, vb = kv(pl.multiple_of(j * tq, tq), tq)
        return tuple(_softmax_step(qcs[c], carries[c], kc, vb, False) for c in range(chains))

    init = tuple((jnp.full((rows, 1), -jnp.inf, F32), jnp.zeros((rows, 1), F32), jnp.zeros((rows, C_HEAD_DIM), F32))
                 for _ in range(chains))
    carries = list(lax.fori_loop(0, qi, below_diagonal, init))
    base = pl.multiple_of(qi * tq, tq)
    for kb in range(chains):
        kc, vb = kv(base + kb * rows, rows)
        for c in range(kb, chains):
            carries[c] = _softmax_step(qcs[c], carries[c], kc, vb, c == kb)
    for c in range(chains):
        _, l, acc = carries[c]
        sl = slice(c * rows, (c + 1) * rows)
        o_ref[sl, :] = (_silu(z_ref[sl, :]) * (acc / l)).astype(o_ref.dtype)


def fox_attn(q, qa, k, ka, v, z):
    b, l, w = q.shape
    chains = ATTN_CHAINS if l % (ATTN_ROWS * ATTN_CHAINS) == 0 else 1
    rows = ATTN_ROWS if l % (ATTN_ROWS * chains) == 0 else l // chains
    tq = rows * chains
    qspec = pl.BlockSpec((None, tq, C_HEAD_DIM), lambda i, h, j: (i, j, h))
    kspec = pl.BlockSpec((None, l, C_HEAD_DIM), lambda i, h, j: (i, 0, h))
    return pl.pallas_call(
        functools.partial(_fox_attn_body, rows=rows, chains=chains),
        grid=(b, C_HEADS, l // tq),
        in_specs=[qspec, qspec, kspec, kspec, kspec, qspec],
        out_specs=qspec,
        out_shape=jax.ShapeDtypeStruct((b, l, w), BF16),
        compiler_params=_params(("parallel", "parallel", "arbitrary")),
        name="fox_attn",
    )(q, qa, k, ka, v, z)


def _fox_attn_sample_body(q_ref, qa_ref, kp_ref, kap_ref, vp_ref, kn_ref, kan_ref, vn_ref, z_ref, o_ref, *, p0):
    n = q_ref.shape[0]
    row = lax.broadcasted_iota(jnp.int32, (n, n), 0)
    col = lax.broadcasted_iota(jnp.int32, (n, n), 1)
    for hd in range(C_HEADS):
        sl = slice(hd * C_HEAD_DIM, (hd + 1) * C_HEAD_DIM)
        head_rows = pl.ds(hd, p0, stride=C_HEADS)
        qc = jnp.concatenate([q_ref[:, sl], qa_ref[:, sl]], axis=1)
        s_past = _dot_nt(qc, jnp.concatenate([kp_ref[head_rows, :].astype(BF16), kap_ref[:, sl]], axis=1))
        s_new = _dot_nt(qc, jnp.concatenate([kn_ref[:, sl], kan_ref[:, sl]], axis=1))
        s_new = jnp.where(col <= row, s_new, -jnp.inf)
        m = jnp.maximum(jnp.max(s_past, axis=-1, keepdims=True), jnp.max(s_new, axis=-1, keepdims=True))
        p_past = jnp.exp2(s_past - m)
        p_new = jnp.exp2(s_new - m)
        l = jnp.sum(p_past, axis=-1, keepdims=True) + jnp.sum(p_new, axis=-1, keepdims=True)
        acc = (_dot(p_past.astype(BF16), vp_ref[head_rows, :].astype(BF16))
               + _dot(p_new.astype(BF16), vn_ref[:, sl]))
        o_ref[:, sl] = (_silu(z_ref[:, sl]) * (acc / l)).astype(o_ref.dtype)


def fox_attn_sample(q, qa_all, cache_k, ka_all, cache_v, layer, k_new, v_new, z):
    b, n, w = q.shape
    p0 = cache_k.shape[2] // C_HEADS
    new = pl.BlockSpec((None, n, w), lambda i: (i, 0, 0))
    new_of_all = pl.BlockSpec((None, n, w), lambda i: (i, p0 // n, 0))
    past_aug = pl.BlockSpec((None, p0, w), lambda i: (i, 0, 0))
    past = pl.BlockSpec((None, None, p0 * C_HEADS, C_HEAD_DIM), lambda i: (layer, i, 0, 0))
    return pl.pallas_call(
        functools.partial(_fox_attn_sample_body, p0=p0),
        grid=(b,),
        in_specs=[new, new_of_all, past, past_aug, past, new, new_of_all, new, new],
        out_specs=new,
        out_shape=jax.ShapeDtypeStruct((b, n, w), BF16),
        compiler_params=_params(("parallel",)),
        name="fox_attn_sample",
    )(q, qa_all, cache_k, ka_all, cache_v, k_new, ka_all, v_new, z)


def _largest_tile(n, cap, mult):
    t = min(n, cap)
    while n % t or t % mult:
        t -= 1
    return t


def _pad_cols(a, width):
    return jnp.pad(a, ((0, 0), (0, width - a.shape[1])))


def _even_weights(w_in, w_out, ws, bs, gv, conv_w, conv_b, dt_bias, a_log, d_skip, g_ssd):
    a_w = A_GROUPS * A_GROUP_DIM
    cuts = (a_w, 2 * a_w, 3 * a_w, 3 * a_w + B_WIDTH, 3 * a_w + B_WIDTH + B_CONV_DIM)
    pieces = jnp.split(w_in, cuts, axis=1)
    pieces[-1] = _pad_cols(pieces[-1], LANES)
    e_mat = np.zeros((LANES, B_HEADS * B_HEAD_DIM), np.float32)
    for h in range(B_HEADS):
        e_mat[h, h * B_HEAD_DIM:(h + 1) * B_HEAD_DIM] = 1.0
    return dict(
        w_in=[p.astype(BF16) for p in pieces], w_out=w_out.astype(BF16), ws=ws, bs=bs, gv=gv,
        cw=conv_w.T, cb=conv_b[None, :], dtb=_pad_cols(dt_bias[None, :], LANES),
        alog=_pad_cols(a_log[None, :], LANES), dskip_e=jnp.repeat(d_skip, B_HEAD_DIM)[None, :],
        gssd=g_ssd[None, :], e_mat=jnp.asarray(e_mat, BF16))


def _even_layer(x, g_pre, g_post, wts, conv0, s0, emit_v):
    b, l, d = x.shape
    t = b * l
    x2 = x.reshape(t, d)
    u, v, za, zb, xbc, dt = norm_proj(x2, g_pre[None, :], wts["w_in"], _largest_tile(t, 256, 8))
    blk = min(l, A_CHUNK)
    r3 = lambda a: a.reshape(b, l, a.shape[1])
    ycat, *vrows, sfin, cfin = even_mixer(
        r3(u), r3(v), r3(za), r3(zb), r3(xbc), r3(dt),
        wts["ws"][:, :blk, :blk], wts["bs"][:, :blk].T, wts["gv"], wts["cw"], wts["cb"], wts["dtb"],
        wts["alog"], wts["dskip_e"], wts["gssd"], wts["e_mat"],
        conv0, s0.reshape(b, B_HEADS * B_HEAD_DIM, B_STATE), blk, emit_v)
    xn = out_proj(ycat.reshape(t, ycat.shape[2]), wts["w_out"], x2, g_post[None, :], _largest_tile(t, 512, 8))
    return (xn.reshape(b, l, d), *vrows, sfin.reshape(b, B_HEADS, B_HEAD_DIM, B_STATE), cfin)


def _odd_weights(w_in, b_f, w_out, gq, gk):
    w = C_HEADS * C_HEAD_DIM
    wq, wk, wv, wz, wf = jnp.split(w_in, (w, 2 * w, 3 * w, 4 * w), axis=1)
    return dict(wq=wq.astype(BF16), wk=wk.astype(BF16), wv=wv.astype(BF16), wz=wz.astype(BF16),
                wf=_pad_cols(wf, LANES).astype(BF16), bf=_pad_cols(b_f[None, :], LANES),
                w_out=w_out.astype(BF16), gq=gq[None, :], gk=gk[None, :])


def _odd_layer(x, g_pre, g_post, wts, n_slots, slot, prev_k, prev_v, cache):
    b, l, d = x.shape
    t = b * l
    x2 = x.reshape(t, d)
    q, kst, kb, vst, vb, z, lf = fox_proj(x2, g_pre[None, :], wts["wq"], wts["wk"], wts["wv"], wts["wz"],
                                          wts["wf"], wts["gq"], wts["gk"], wts["bf"], _largest_tile(t, 256, 16),
                                          n_slots, slot, prev_k, prev_v)
    r3 = lambda a: a.reshape(b, l, a.shape[1])
    lf3 = r3(lf)
    if cache is None:
        qa, ka = fox_aug(lf3, _largest_tile(l, 512, 16))
        yc = fox_attn(r3(q), qa, r3(kb), ka, r3(vb), r3(z))
    else:
        cache_k, cache_v, cache_logf = cache
        p0 = cache_k.shape[2]
        past_lf = _pad_cols(cache_logf[slot].reshape(b * p0, C_HEADS), LANES).reshape(b, p0, LANES)
        qa, ka = fox_aug(jnp.concatenate([past_lf, lf3], axis=1), _largest_tile(p0 + l, 512, 16))
        rows = lambda c: c.reshape(c.shape[0], b, p0 * C_HEADS, C_HEAD_DIM)
        yc = fox_attn_sample(r3(q), qa, rows(cache_k), ka, rows(cache_v), slot, r3(kb), r3(vb), r3(z))
    xn = out_proj(yc.reshape(t, d), wts["w_out"], x2, g_post[None, :], _largest_tile(t, 512, 8))
    return xn.reshape(b, l, d), kst, vst, lf3[:, :, :C_HEADS]


def kernel(x_prompt, x_sample, cache_fox_k, cache_fox_v, cache_fox_logf, state_ssd, state_conv, norm_pre, norm_post, w_in_even, w_out_even, gmlp_ws, gmlp_bs, gmlp_gv, ssd_conv_w, ssd_conv_b, ssd_dt_bias, ssd_a_log, ssd_d, ssd_norm_g, w_in_odd, fox_b_forget, w_out_odd, fox_gq, fox_gk):
    xp, xs = x_prompt, x_sample
    bp = xp.shape[0]
    depth = norm_pre.shape[0]
    n_odd = depth // 2
    outs = {n: [] for n in ("lp", "sp", "cp", "ls", "ss", "cs", "gs")}
    kp = vp = ks = vs = None
    cache = (cache_fox_k, cache_fox_v, cache_fox_logf)
    for i in range(depth):
        j = i // 2
        if i % 2 == 0:
            wts = _even_weights(w_in_even[j], w_out_even[j], gmlp_ws[j], gmlp_bs[j], gmlp_gv[j], ssd_conv_w[j],
                                ssd_conv_b[j], ssd_dt_bias[j], ssd_a_log[j], ssd_d[j], ssd_norm_g[j])
            xp, sp, cp = _even_layer(xp, norm_pre[i], norm_post[i], wts,
                                     jnp.zeros((bp, B_CONV - 1, B_CONV_DIM), F32),
                                     jnp.zeros((bp, B_HEADS, B_HEAD_DIM, B_STATE), F32), False)
            xs, gs, ss, cs = _even_layer(xs, norm_pre[i], norm_post[i], wts, state_conv[j], state_ssd[j], True)
            for n, a in (("sp", sp), ("cp", cp), ("ss", ss), ("cs", cs), ("gs", gs)):
                outs[n].append(a)
        else:
            wts = _odd_weights(w_in_odd[j], fox_b_forget[j], w_out_odd[j], fox_gq[j], fox_gk[j])
            xp, kp, vp, lp = _odd_layer(xp, norm_pre[i], norm_post[i], wts, n_odd, j, kp, vp, None)
            xs, ks, vs, ls = _odd_layer(xs, norm_pre[i], norm_post[i], wts, n_odd, j, ks, vs, cache)
            outs["lp"].append(lp)
            outs["ls"].append(ls)
    st = {n: jnp.stack(a) for n, a in outs.items()}
    rows5 = lambda a, x: a.reshape(n_odd, x.shape[0], x.shape[1], C_HEADS, C_HEAD_DIM)
    return (xp, xs, rows5(kp, xp), rows5(vp, xp), st["lp"], st["sp"], st["cp"],
            rows5(ks, xs), rows5(vs, xs), st["ls"], st["ss"], st["cs"], st["gs"])
```

```python
import functools

import numpy as np
import jax
import jax.numpy as jnp
from jax import lax
from jax.experimental import pallas as pl
from jax.experimental.pallas import tpu as pltpu

F32 = jnp.float32
BF16 = jnp.bfloat16
EPS = 1e-6
LANES = 128
VMEM_LIMIT = 56 * 1024 * 1024

D_MODEL = 1024
A_GROUPS = 4
A_GROUP_DIM = 256
A_WIDTH = A_GROUPS * A_GROUP_DIM
A_CHUNK = 128
SUB_CHUNK = 64
B_HEADS = 16
B_HEAD_DIM = 64
B_GROUPS = 2
B_STATE = 128
B_CONV = 4
B_WIDTH = 1024
B_CONV_DIM = B_WIDTH + 2 * B_GROUPS * B_STATE
CTX_ROW = 8
C_HEADS = 8
C_HEAD_DIM = 128
SQRT_HALF = 0.7071067811865476
LOG2E = 1.4426950408889634
AUG_PIECES = 3
AUG_HALF = AUG_PIECES * C_HEADS
MIXER_BLOCKS_PER_STEP = 4


def _params(sem):
    return pltpu.CompilerParams(dimension_semantics=sem, vmem_limit_bytes=VMEM_LIMIT)


def _const_spec(shape):
    nd = len(shape)
    return pl.BlockSpec(shape, lambda *_: (0,) * nd)


def _dot(a, b):
    return jnp.dot(a, b, preferred_element_type=F32)


def _dot_nt(a, b):
    return lax.dot_general(a, b, (((1,), (1,)), ((), ())), preferred_element_type=F32)


def _split3(x):
    hi = x.astype(BF16)
    r = x - hi.astype(F32)
    mid = r.astype(BF16)
    lo = (r - mid.astype(F32)).astype(BF16)
    return hi, mid, lo


def _dot3_lhs(x, w):
    hi, mid, lo = _split3(x)
    return _dot(hi, w) + _dot(mid, w) + _dot(lo, w)


def _dot3_rhs(w, x):
    hi, mid, lo = _split3(x)
    return _dot(w, hi) + _dot(w, mid) + _dot(w, lo)


def _rms(x, g):
    return x * lax.rsqrt(jnp.mean(x * x, axis=-1, keepdims=True) + EPS) * g


def _gelu(x):
    return 0.5 * x * (1.0 + lax.erf(x * SQRT_HALF))


def _silu(x):
    hx = 0.5 * x
    return hx + hx * jnp.tanh(hx)


def _tril(n):
    row = lax.broadcasted_iota(jnp.int32, (n, n), 0)
    col = lax.broadcasted_iota(jnp.int32, (n, n), 1)
    return col <= row


def _even_proj_body(x_ref, g_ref, wu_ref, wv_ref, wza_ref, wzb_ref, wx_ref, wdt_ref, gv_ref, cw_ref, cb_ref,
                    dtb_ref, conv0_ref, *refs, tm, emit_v):
    gate_ref, vb_ref, zs_ref, xc_ref, dt_ref, cfin_ref, prev_sc = refs[:2] + refs[-5:]
    j = pl.program_id(1)
    h = _rms(x_ref[...], g_ref[...]).astype(BF16)
    gate_ref[...] = _silu(_dot(h, wza_ref[...])) * _gelu(_dot(h, wu_ref[...]))
    gv = _gelu(_dot(h, wv_ref[...]))
    vn = jnp.concatenate(
        [_rms(gv[:, g * A_GROUP_DIM:(g + 1) * A_GROUP_DIM], gv_ref[g:g + 1, :]) for g in range(A_GROUPS)], axis=1)
    vb_ref[...] = vn.astype(BF16)
    if emit_v:
        refs[2][...] = vn
    zs_ref[...] = _silu(_dot(h, wzb_ref[...]))
    dt_ref[...] = jax.nn.softplus(_dot(h, wdt_ref[...]) + dtb_ref[...])

    @pl.when(j == 0)
    def _():
        prev_sc[...] = jnp.zeros_like(prev_sc)
        prev_sc[CTX_ROW - (B_CONV - 1):CTX_ROW, :] = conv0_ref[...]

    xbc = _dot(h, wx_ref[...])
    ext = jnp.concatenate([prev_sc[...], xbc], axis=0)
    acc = cb_ref[...] + xbc * cw_ref[B_CONV - 1:B_CONV, :]
    for back in range(1, B_CONV):
        acc = acc + pltpu.roll(ext, back, axis=0)[CTX_ROW:, :] * cw_ref[B_CONV - 1 - back:B_CONV - back, :]
    xc_ref[...] = _silu(acc)
    prev_sc[...] = xbc[tm - CTX_ROW:, :]

    @pl.when(j == pl.num_programs(1) - 1)
    def _():
        cfin_ref[...] = prev_sc[CTX_ROW - (B_CONV - 1):CTX_ROW, :]


def even_proj(x, g, ws, gv, cw, cb, dtb, conv0, tm, emit_v):
    b, l, d = x.shape
    tok = lambda w: pl.BlockSpec((None, tm, w), lambda i, j: (i, j, 0))
    ctx = pl.BlockSpec((None, B_CONV - 1, B_CONV_DIM), lambda i, j: (i, 0, 0))
    consts = [g] + list(ws) + [gv, cw, cb, dtb]
    act = lambda w, dt: jax.ShapeDtypeStruct((b, l, w), dt)
    v_spec, v_shape = ([tok(A_WIDTH)], [act(A_WIDTH, F32)]) if emit_v else ([], [])
    return pl.pallas_call(
        functools.partial(_even_proj_body, tm=tm, emit_v=emit_v),
        grid=(b, l // tm),
        in_specs=[tok(d)] + [_const_spec(c.shape) for c in consts] + [ctx],
        out_specs=[tok(A_WIDTH), tok(A_WIDTH)] + v_spec + [tok(B_WIDTH), tok(B_CONV_DIM), tok(LANES), ctx],
        out_shape=[act(A_WIDTH, F32), act(A_WIDTH, BF16)] + v_shape
        + [act(B_WIDTH, F32), act(B_CONV_DIM, F32), act(LANES, F32),
           jax.ShapeDtypeStruct((b, B_CONV - 1, B_CONV_DIM), F32)],
        scratch_shapes=[pltpu.VMEM((CTX_ROW, B_CONV_DIM), F32)],
        compiler_params=_params(("parallel", "arbitrary")),
        name="even_proj",
    )(x, *consts, conv0)


def _out_proj_body(y_ref, w_ref, x_ref, g_ref, o_ref):
    o = _dot(y_ref[...], w_ref[...])
    o_ref[...] = x_ref[...] + _rms(o, g_ref[...])


def out_proj(y, w, x, g, tm):
    t, d = x.shape
    kd = y.shape[1]
    return pl.pallas_call(
        _out_proj_body,
        grid=(t // tm,),
        in_specs=[pl.BlockSpec((tm, kd), lambda i: (i, 0)), _const_spec(w.shape),
                  pl.BlockSpec((tm, d), lambda i: (i, 0)), _const_spec((1, d))],
        out_specs=pl.BlockSpec((tm, d), lambda i: (i, 0)),
        out_shape=jax.ShapeDtypeStruct((t, d), F32),
        compiler_params=_params(("parallel",)),
        name="out_proj",
    )(y, w, x, g)


def _even_mixer_body(gate_ref, vb_ref, zs_ref, xc_ref, dt_ref, ws_ref, bs_ref, alog_ref, dskip_ref, gssd_ref,
                     e_ref, s0_ref, y_ref, sfin_ref, st_sc, *, blk, nsub):
    j = pl.program_id(1)

    @pl.when(j == 0)
    def _():
        st_sc[...] = s0_ref[...].T

    for sub in range(nsub):
        _mixer_block(slice(sub * blk, (sub + 1) * blk), blk, gate_ref, vb_ref, zs_ref, xc_ref, dt_ref, ws_ref,
                     bs_ref, alog_ref, dskip_ref, gssd_ref, e_ref, y_ref, st_sc)

    @pl.when(j == pl.num_programs(1) - 1)
    def _():
        sfin_ref[...] = st_sc[...].T


def _mixer_block(rs, blk, gate_ref, vb_ref, zs_ref, xc_ref, dt_ref, ws_ref, bs_ref, alog_ref, dskip_ref, gssd_ref,
                 e_ref, y_ref, st_sc):
    row = lax.broadcasted_iota(jnp.int32, (blk, blk), 0)
    col = lax.broadcasted_iota(jnp.int32, (blk, blk), 1)
    causal = col <= row

    chunk_causal = (col // SUB_CHUNK) <= (row // SUB_CHUNK)
    s_parts = []
    for g in range(A_GROUPS):
        wg = jnp.where(chunk_causal, ws_ref[g], 0.0).astype(BF16)
        sg = _dot(wg, vb_ref[rs, g * A_GROUP_DIM:(g + 1) * A_GROUP_DIM])
        s_parts.append(sg + bs_ref[:, g:g + 1])
    ya = gate_ref[rs, :] * jnp.concatenate(s_parts, axis=1)

    xs = xc_ref[rs, :B_WIDTH]
    bm = xc_ref[rs, B_WIDTH:B_WIDTH + B_GROUPS * B_STATE]
    cm = xc_ref[rs, B_WIDTH + B_GROUPS * B_STATE:]
    lane = lax.broadcasted_iota(jnp.int32, (1, LANES), 1)
    dt = dt_ref[rs, :]
    a = jnp.where(lane < B_HEADS, -jnp.exp(alog_ref[...]), 0.0)
    acs = _dot3_rhs(causal.astype(BF16), dt * a)
    acs_t = acs.T
    e = e_ref[...]
    dt_e = _dot3_lhs(dt, e)
    acs_e = _dot3_lhs(acs, e)
    expacs_e = jnp.exp(acs_e)
    toend_e = jnp.exp(acs_e[blk - 1:blk, :] - acs_e)
    blkdec_e = expacs_e[blk - 1:blk, :]
    dtx = dt_e * xs
    wds = (toend_e * dtx).astype(BF16)
    dtx_b = dtx.astype(BF16)
    lane_hp = lax.broadcasted_iota(jnp.int32, (blk, LANES), 1)
    gw = (B_HEADS // B_GROUPS) * B_HEAD_DIM
    ys_parts, yi_parts = [], []
    for g in range(B_GROUPS):
        bm_g = bm[:, g * B_STATE:(g + 1) * B_STATE]
        cm_g = cm[:, g * B_STATE:(g + 1) * B_STATE].astype(BF16)
        cb = _dot_nt(cm_g, bm_g.astype(BF16))
        st_g = st_sc[:, g * gw:(g + 1) * gw]
        ds = _dot(bm_g.T.astype(BF16), wds[:, g * gw:(g + 1) * gw])
        st_sc[:, g * gw:(g + 1) * gw] = blkdec_e[:, g * gw:(g + 1) * gw] * st_g + ds
        ys_parts.append(_dot(cm_g, st_g.astype(BF16)))
        for hp in range(gw // LANES):
            c0 = g * gw + hp * LANES
            rhs = dtx_b[:, c0:c0 + LANES]
            pair = None
            for half in range(2):
                h = c0 // B_HEAD_DIM + half
                seg = acs[:, h:h + 1] - acs_t[h:h + 1, :]
                decay = jnp.exp(jnp.where(causal, seg, -jnp.inf))
                m_h = (cb * decay).astype(BF16)
                keep = (lane_hp < B_HEAD_DIM) if half == 0 else (lane_hp >= B_HEAD_DIM)
                part = _dot(m_h, jnp.where(keep, rhs, jnp.zeros_like(rhs)))
                pair = part if pair is None else pair + part
            yi_parts.append(pair)
    y = (jnp.concatenate(yi_parts, axis=1) + jnp.concatenate(ys_parts, axis=1) * expacs_e
         + dskip_ref[...] * xs)
    y = y * zs_ref[rs, :]
    half_w = B_WIDTH // B_GROUPS
    yb = jnp.concatenate(
        [_rms(y[:, g * half_w:(g + 1) * half_w], gssd_ref[:, g * half_w:(g + 1) * half_w])
         for g in range(B_GROUPS)], axis=1)
    y_ref[rs, :] = jnp.concatenate([ya, yb], axis=1).astype(y_ref.dtype)


def even_mixer(gate, vb, zs, xc, dt, ws, bs_t, alog, dskip_e, gssd, e_mat, s0, blk):
    b, l, _ = gate.shape
    nsub = MIXER_BLOCKS_PER_STEP if l % (MIXER_BLOCKS_PER_STEP * blk) == 0 else 1
    tok = lambda w: pl.BlockSpec((None, nsub * blk, w), lambda i, j: (i, j, 0))
    hp = B_HEADS * B_HEAD_DIM
    state = pl.BlockSpec((None, hp, B_STATE), lambda i, j: (i, 0, 0))
    consts = [ws, bs_t, alog, dskip_e, gssd, e_mat]
    return pl.pallas_call(
        functools.partial(_even_mixer_body, blk=blk, nsub=nsub),
        grid=(b, l // (nsub * blk)),
        in_specs=[tok(A_WIDTH), tok(A_WIDTH), tok(B_WIDTH), tok(B_CONV_DIM), tok(LANES)]
        + [_const_spec(c.shape) for c in consts] + [state],
        out_specs=[tok(A_WIDTH + B_WIDTH), state],
        out_shape=[jax.ShapeDtypeStruct((b, l, A_WIDTH + B_WIDTH), BF16), jax.ShapeDtypeStruct((b, hp, B_STATE), F32)],
        scratch_shapes=[pltpu.VMEM((B_STATE, hp), F32)],
        compiler_params=_params(("parallel", "arbitrary")),
        name="even_mixer",
    )(gate, vb, zs, xc, dt, *consts, s0)


def _forget_cumsum(lf, carry_sc, first):
    n = lf.shape[0]

    @pl.when(first)
    def _():
        carry_sc[...] = jnp.zeros_like(carry_sc)

    lane = lax.broadcasted_iota(jnp.int32, (1, LANES), 1)
    fc = _dot3_rhs(_tril(n).astype(BF16), jnp.where(lane < C_HEADS, lf, 0.0)) + carry_sc[...]
    carry_sc[...] = fc[n - 1:n, :]
    return fc


def _forget_lanes(fc):
    lane = lax.broadcasted_iota(jnp.int32, (1, LANES), 1)
    hi, mid, lo = _split3(fc * LOG2E)
    pieces = (hi.astype(F32) + pltpu.roll(mid.astype(F32), C_HEADS, axis=1)
              + pltpu.roll(lo.astype(F32), 2 * C_HEADS, axis=1))
    q_side = pieces + ((lane >= AUG_HALF) & (lane < 2 * AUG_HALF)).astype(F32)
    k_side = (lane < AUG_HALF).astype(F32) - pltpu.roll(pieces, AUG_HALF, axis=1)
    k_heads = [jnp.where((lane % C_HEADS == h) & (lane < 2 * AUG_HALF), k_side, 0.0) for h in range(C_HEADS)]
    return q_side.astype(BF16), jnp.concatenate(k_heads, axis=1).astype(BF16)


def _fox_proj_body(x_ref, g_ref, wq_ref, wk_ref, wv_ref, wz_ref, wf_ref, gq_ref, gk_ref, bf_ref, *refs,
                   slot, first, with_aug):
    outs = refs[2:] if not first else refs
    q_ref, kf_ref, kb_ref, vf_ref, vb_ref, z_ref, lf_ref = outs[:7]
    tm = x_ref.shape[0]
    if first:
        kf_ref[...] = jnp.zeros(kf_ref.shape, F32)
        vf_ref[...] = jnp.zeros(vf_ref.shape, F32)
        kf_ref, vf_ref = kf_ref.at[slot], vf_ref.at[slot]
    h = _rms(x_ref[...], g_ref[...]).astype(BF16)
    q = _dot(h, wq_ref[...])
    k = _dot(h, wk_ref[...])
    v = _dot(h, wv_ref[...])
    q_scale = C_HEAD_DIM ** -0.5 * LOG2E
    kn = []
    for hd in range(C_HEADS):
        sl = slice(hd * C_HEAD_DIM, (hd + 1) * C_HEAD_DIM)
        q_ref[:, sl] = (_rms(q[:, sl], gq_ref[...]) * q_scale).astype(BF16)
        kn.append(_rms(k[:, sl], gk_ref[...]))
    kn = jnp.concatenate(kn, axis=1)
    kf_ref[...] = kn.reshape(tm, C_HEADS, C_HEAD_DIM)
    kb_ref[...] = kn.astype(BF16)
    vf_ref[...] = v.reshape(tm, C_HEADS, C_HEAD_DIM)
    vb_ref[...] = v.astype(BF16)
    z_ref[...] = _dot(h, wz_ref[...])
    lf = jax.nn.log_sigmoid(_dot(h, wf_ref[...]) + bf_ref[...])
    lf_ref[...] = lf
    if with_aug:
        qa_ref, ka_ref, carry_sc = outs[7:]
        qa_ref[...], ka_ref[...] = _forget_lanes(_forget_cumsum(lf, carry_sc, pl.program_id(1) == 0))


def fox_proj(x, g, wq, wk, wv, wz, wf, gq, gk, bf, tm, n_slots, slot, prev_k, prev_v, with_aug):
    b, l, d = x.shape
    tok = lambda w: pl.BlockSpec((None, tm, w), lambda i, j: (i, j, 0))
    act = lambda w, dt: jax.ShapeDtypeStruct((b, l, w), dt)
    ins = [g, wq, wk, wv, wz, wf, gq, gk, bf]
    first = prev_k is None
    stacked = jax.ShapeDtypeStruct((n_slots, b, l, C_HEADS, C_HEAD_DIM), F32)
    if first:
        st_spec = pl.BlockSpec((n_slots, None, tm, C_HEADS, C_HEAD_DIM), lambda i, j: (0, i, j, 0, 0))
        extra, extra_specs, aliases = [], [], {}
    else:
        st_spec = pl.BlockSpec((None, None, tm, C_HEADS, C_HEAD_DIM), lambda i, j: (slot, i, j, 0, 0))
        extra, extra_specs = [prev_k, prev_v], [pl.BlockSpec(memory_space=pl.ANY)] * 2
        aliases = {1 + len(ins): 1, 2 + len(ins): 3}
    w = C_HEADS * C_HEAD_DIM
    aug_specs, aug_shapes, scratch = [], [], []
    if with_aug:
        aug_specs, aug_shapes = [tok(LANES), tok(w)], [act(LANES, BF16), act(w, BF16)]
        scratch = [pltpu.VMEM((1, LANES), F32)]
    return pl.pallas_call(
        functools.partial(_fox_proj_body, slot=slot, first=first, with_aug=with_aug),
        grid=(b, l // tm),
        in_specs=[tok(d)] + [_const_spec(a.shape) for a in ins] + extra_specs,
        out_specs=[tok(w), st_spec, tok(w), st_spec, tok(w), tok(w), tok(LANES)] + aug_specs,
        out_shape=[act(w, BF16), stacked, act(w, BF16), stacked, act(w, BF16), act(w, F32), act(LANES, F32)]
        + aug_shapes,
        input_output_aliases=aliases,
        scratch_shapes=scratch,
        compiler_params=_params(("parallel", "arbitrary")),
        name="fox_proj",
    )(x, *ins, *extra)


def _fox_aug_body(lf_ref, qa_ref, ka_ref, carry_sc):
    qa_ref[...], ka_ref[...] = _forget_lanes(_forget_cumsum(lf_ref[...], carry_sc, pl.program_id(1) == 0))


def fox_aug(lf, ts):
    b, s, _ = lf.shape
    w = C_HEADS * C_HEAD_DIM
    return pl.pallas_call(
        _fox_aug_body,
        grid=(b, s // ts),
        in_specs=[pl.BlockSpec((None, ts, LANES), lambda i, j: (i, j, 0))],
        out_specs=[pl.BlockSpec((None, ts, LANES), lambda i, j: (i, j, 0)),
                   pl.BlockSpec((None, ts, w), lambda i, j: (i, j, 0))],
        out_shape=[jax.ShapeDtypeStruct((b, s, LANES), BF16), jax.ShapeDtypeStruct((b, s, w), BF16)],
        scratch_shapes=[pltpu.VMEM((1, LANES), F32)],
        compiler_params=_params(("parallel", "arbitrary")),
        name="fox_aug",
    )(lf)


ATTN_ROWS = 1024
ATTN_CHAINS = 2


def _softmax_step(qc, carry, kc, vb, diagonal):
    m, l, acc = carry
    s = _dot_nt(qc, kc)
    if diagonal:
        s = jnp.where(_tril(s.shape[0]), s, -jnp.inf)
    m_new = jnp.maximum(m, jnp.max(s, axis=-1, keepdims=True))
    alpha = jnp.exp2(m - m_new)
    p = jnp.exp2(s - m_new)
    l = alpha * l + jnp.sum(p, axis=-1, keepdims=True)
    acc = alpha * acc + _dot(p.astype(BF16), vb)
    return m_new, l, acc


def _fox_attn_body(q_ref, qa_ref, k_ref, ka_ref, v_ref, z_ref, o_ref, *, rows, chains):
    qi = pl.program_id(2)
    tq = rows * chains
    qcs = [jnp.concatenate([q_ref[c * rows:(c + 1) * rows, :], qa_ref[c * rows:(c + 1) * rows, :]], axis=1)
           for c in range(chains)]

    def kv(r0, n):
        return (jnp.concatenate([k_ref[pl.ds(r0, n), :], ka_ref[pl.ds(r0, n), :]], axis=1), v_ref[pl.ds(r0, n), :])

    def below_diagonal(j, carries):
        kc, vb = kv(pl.multiple_of(j * tq, tq), tq)
        return tuple(_softmax_step(qcs[c], carries[c], kc, vb, False) for c in range(chains))

    init = tuple((jnp.full((rows, 1), -jnp.inf, F32), jnp.zeros((rows, 1), F32), jnp.zeros((rows, C_HEAD_DIM), F32))
                 for _ in range(chains))
    carries = list(lax.fori_loop(0, qi, below_diagonal, init))
    base = pl.multiple_of(qi * tq, tq)
    for kb in range(chains):
        kc, vb = kv(base + kb * rows, rows)
        for c in range(kb, chains):
            carries[c] = _softmax_step(qcs[c], carries[c], kc, vb, c == kb)
    for c in range(chains):
        _, l, acc = carries[c]
        sl = slice(c * rows, (c + 1) * rows)
        o_ref[sl, :] = (_silu(z_ref[sl, :]) * (acc / l)).astype(o_ref.dtype)


def fox_attn(q, qa, k, ka, v, z):
    b, l, w = q.shape
    chains = ATTN_CHAINS if l % (ATTN_ROWS * ATTN_CHAINS) == 0 else 1
    rows = ATTN_ROWS if l % (ATTN_ROWS * chains) == 0 else l // chains
    tq = rows * chains
    qspec = pl.BlockSpec((None, tq, C_HEAD_DIM), lambda i, h, j: (i, j, h))
    qaspec = pl.BlockSpec((None, tq, LANES), lambda i, h, j: (i, j, 0))
    kspec = pl.BlockSpec((None, l, C_HEAD_DIM), lambda i, h, j: (i, 0, h))
    return pl.pallas_call(
        functools.partial(_fox_attn_body, rows=rows, chains=chains),
        grid=(b, C_HEADS, l // tq),
        in_specs=[qspec, qaspec, kspec, kspec, kspec, qspec],
        out_specs=qspec,
        out_shape=jax.ShapeDtypeStruct((b, l, w), BF16),
        compiler_params=_params(("parallel", "parallel", "arbitrary")),
        name="fox_attn",
    )(q, qa, k, ka, v, z)


def _fox_attn_sample_body(q_ref, qa_ref, kp_ref, kap_ref, vp_ref, kn_ref, kan_ref, vn_ref, z_ref, o_ref, *, p0):
    n = q_ref.shape[0]
    for hd in range(C_HEADS):
        sl = slice(hd * C_HEAD_DIM, (hd + 1) * C_HEAD_DIM)
        head_rows = pl.ds(hd, p0, stride=C_HEADS)
        qc = jnp.concatenate([q_ref[:, sl], qa_ref[...]], axis=1)
        s_past = _dot_nt(qc, jnp.concatenate([kp_ref[head_rows, :].astype(BF16), kap_ref[:, sl]], axis=1))
        s_new = _dot_nt(qc, jnp.concatenate([kn_ref[:, sl], kan_ref[:, sl]], axis=1))
        s_new = jnp.where(_tril(n), s_new, -jnp.inf)
        m = jnp.maximum(jnp.max(s_past, axis=-1, keepdims=True), jnp.max(s_new, axis=-1, keepdims=True))
        p_past = jnp.exp2(s_past - m)
        p_new = jnp.exp2(s_new - m)
        l = jnp.sum(p_past, axis=-1, keepdims=True) + jnp.sum(p_new, axis=-1, keepdims=True)
        acc = (_dot(p_past.astype(BF16), vp_ref[head_rows, :].astype(BF16))
               + _dot(p_new.astype(BF16), vn_ref[:, sl]))
        o_ref[:, sl] = (_silu(z_ref[:, sl]) * (acc / l)).astype(o_ref.dtype)


def fox_attn_sample(q, qa_all, cache_k, ka_all, cache_v, layer, k_new, v_new, z):
    b, n, w = q.shape
    p0 = cache_k.shape[2] // C_HEADS
    new = pl.BlockSpec((None, n, w), lambda i: (i, 0, 0))
    new_of_all = lambda width: pl.BlockSpec((None, n, width), lambda i: (i, p0 // n, 0))
    past_aug = pl.BlockSpec((None, p0, w), lambda i: (i, 0, 0))
    past = pl.BlockSpec((None, None, p0 * C_HEADS, C_HEAD_DIM), lambda i: (layer, i, 0, 0))
    return pl.pallas_call(
        functools.partial(_fox_attn_sample_body, p0=p0),
        grid=(b,),
        in_specs=[new, new_of_all(LANES), past, past_aug, past, new, new_of_all(w), new, new],
        out_specs=new,
        out_shape=jax.ShapeDtypeStruct((b, n, w), BF16),
        compiler_params=_params(("parallel",)),
        name="fox_attn_sample",
    )(q, qa_all, cache_k, ka_all, cache_v, k_new, ka_all, v_new, z)


def _largest_tile(n, cap, mult):
    t = min(n, cap)
    while n % t or t % mult:
        t -= 1
    return t


def _pad_cols(a, width):
    return jnp.pad(a, ((0, 0), (0, width - a.shape[1])))


def _even_weights(w_in, w_out, ws, bs, gv, conv_w, conv_b, dt_bias, a_log, d_skip, g_ssd):
    cuts = (A_WIDTH, 2 * A_WIDTH, 3 * A_WIDTH, 3 * A_WIDTH + B_WIDTH, 3 * A_WIDTH + B_WIDTH + B_CONV_DIM)
    wu, wv, wza, wzb, wx, wdt = jnp.split(w_in, cuts, axis=1)
    e_mat = np.zeros((LANES, B_HEADS * B_HEAD_DIM), np.float32)
    for h in range(B_HEADS):
        e_mat[h, h * B_HEAD_DIM:(h + 1) * B_HEAD_DIM] = 1.0
    return dict(
        w_in=[p.astype(BF16) for p in (wu, wv, wza, wzb, wx, _pad_cols(wdt, LANES))], w_out=w_out.astype(BF16),
        ws=ws, bs=bs, gv=gv, cw=conv_w.T, cb=conv_b[None, :], dtb=_pad_cols(dt_bias[None, :], LANES),
        alog=_pad_cols(a_log[None, :], LANES), dskip_e=jnp.repeat(d_skip, B_HEAD_DIM)[None, :],
        gssd=g_ssd[None, :], e_mat=jnp.asarray(e_mat, BF16))


def _even_layer(x, g_pre, g_post, wts, conv0, s0, emit_v):
    b, l, d = x.shape
    t = b * l
    gate, vb, *vrows, zs, xc, dt, cfin = even_proj(
        x, g_pre[None, :], wts["w_in"], wts["gv"], wts["cw"], wts["cb"], wts["dtb"], conv0,
        _largest_tile(l, 256, 16), emit_v)
    blk = min(l, A_CHUNK)
    ycat, sfin = even_mixer(gate, vb, zs, xc, dt, wts["ws"][:, :blk, :blk], wts["bs"][:, :blk].T, wts["alog"],
                            wts["dskip_e"], wts["gssd"], wts["e_mat"],
                            s0.reshape(b, B_HEADS * B_HEAD_DIM, B_STATE), blk)
    xn = out_proj(ycat.reshape(t, ycat.shape[2]), wts["w_out"], x.reshape(t, d), g_post[None, :],
                  _largest_tile(t, 512, 8))
    return (xn.reshape(b, l, d), *vrows, sfin.reshape(b, B_HEADS, B_HEAD_DIM, B_STATE), cfin)


def _odd_weights(w_in, b_f, w_out, gq, gk):
    w = C_HEADS * C_HEAD_DIM
    wq, wk, wv, wz, wf = jnp.split(w_in, (w, 2 * w, 3 * w, 4 * w), axis=1)
    return dict(wq=wq.astype(BF16), wk=wk.astype(BF16), wv=wv.astype(BF16), wz=wz.astype(BF16),
                wf=_pad_cols(wf, LANES).astype(BF16), bf=_pad_cols(b_f[None, :], LANES),
                w_out=w_out.astype(BF16), gq=gq[None, :], gk=gk[None, :])


def _odd_layer(x, g_pre, g_post, wts, n_slots, slot, prev_k, prev_v, cache):
    b, l, d = x.shape
    t = b * l
    prompt = cache is None
    q, kst, kb, vst, vb, z, lf, *aug = fox_proj(
        x, g_pre[None, :], wts["wq"], wts["wk"], wts["wv"], wts["wz"], wts["wf"], wts["gq"], wts["gk"], wts["bf"],
        _largest_tile(l, 256, 16), n_slots, slot, prev_k, prev_v, prompt)
    if prompt:
        yc = fox_attn(q, aug[0], kb, aug[1], vb, z)
    else:
        cache_k, cache_v, cache_logf = cache
        p0 = cache_k.shape[2]
        past_lf = _pad_cols(cache_logf[slot].reshape(b * p0, C_HEADS), LANES).reshape(b, p0, LANES)
        qa, ka = fox_aug(jnp.concatenate([past_lf, lf], axis=1), _largest_tile(p0 + l, 1040, 16))
        rows = lambda c: c.reshape(c.shape[0], b, p0 * C_HEADS, C_HEAD_DIM)
        yc = fox_attn_sample(q, qa, rows(cache_k), ka, rows(cache_v), slot, kb, vb, z)
    xn = out_proj(yc.reshape(t, d), wts["w_out"], x.reshape(t, d), g_post[None, :], _largest_tile(t, 512, 8))
    return xn.reshape(b, l, d), kst, vst, lf[:, :, :C_HEADS]


def kernel(x_prompt, x_sample, cache_fox_k, cache_fox_v, cache_fox_logf, state_ssd, state_conv, norm_pre, norm_post, w_in_even, w_out_even, gmlp_ws, gmlp_bs, gmlp_gv, ssd_conv_w, ssd_conv_b, ssd_dt_bias, ssd_a_log, ssd_d, ssd_norm_g, w_in_odd, fox_b_forget, w_out_odd, fox_gq, fox_gk):
    xp, xs = x_prompt, x_sample
    bp = xp.shape[0]
    depth = norm_pre.shape[0]
    n_odd = depth // 2
    outs = {n: [] for n in ("lp", "sp", "cp", "ls", "ss", "cs", "gs")}
    kp = vp = ks = vs = None
    cache = (cache_fox_k, cache_fox_v, cache_fox_logf)
    for i in range(depth):
        j = i // 2
        if i % 2 == 0:
            wts = _even_weights(w_in_even[j], w_out_even[j], gmlp_ws[j], gmlp_bs[j], gmlp_gv[j], ssd_conv_w[j],
                                ssd_conv_b[j], ssd_dt_bias[j], ssd_a_log[j], ssd_d[j], ssd_norm_g[j])
            xp, sp, cp = _even_layer(xp, norm_pre[i], norm_post[i], wts,
                                     jnp.zeros((bp, B_CONV - 1, B_CONV_DIM), F32),
                                     jnp.zeros((bp, B_HEADS, B_HEAD_DIM, B_STATE), F32), False)
            xs, gs, ss, cs = _even_layer(xs, norm_pre[i], norm_post[i], wts, state_conv[j], state_ssd[j], True)
            for n, a in (("sp", sp), ("cp", cp), ("ss", ss), ("cs", cs), ("gs", gs)):
                outs[n].append(a)
        else:
            wts = _odd_weights(w_in_odd[j], fox_b_forget[j], w_out_odd[j], fox_gq[j], fox_gk[j])
            xp, kp, vp, lp = _odd_layer(xp, norm_pre[i], norm_post[i], wts, n_odd, j, kp, vp, None)
            xs, ks, vs, ls = _odd_layer(xs, norm_pre[i], norm_post[i], wts, n_odd, j, ks, vs, cache)
            outs["lp"].append(lp)
            outs["ls"].append(ls)
    st = {n: jnp.stack(a) for n, a in outs.items()}
    return (xp, xs, kp, vp, st["lp"], st["sp"], st["cp"], ks, vs, st["ls"], st["ss"], st["cs"], st["gs"])
```

```python
import functools

import numpy as np
import jax
import jax.numpy as jnp
from jax import lax
from jax.experimental import pallas as pl
from jax.experimental.pallas import tpu as pltpu

F32 = jnp.float32
BF16 = jnp.bfloat16
EPS = 1e-6
LANES = 128
VMEM_LIMIT = 56 * 1024 * 1024

D_MODEL = 1024
A_GROUPS = 4
A_GROUP_DIM = 256
A_WIDTH = A_GROUPS * A_GROUP_DIM
A_CHUNK = 128
SUB_CHUNK = 64
B_HEADS = 16
B_HEAD_DIM = 64
B_GROUPS = 2
B_STATE = 128
B_CONV = 4
B_WIDTH = 1024
B_CONV_DIM = B_WIDTH + 2 * B_GROUPS * B_STATE
CTX_ROW = 8
C_HEADS = 8
C_HEAD_DIM = 128
SQRT_HALF = 0.7071067811865476
LOG2E = 1.4426950408889634
AUG_PIECES = 3
AUG_HALF = AUG_PIECES * C_HEADS
MIXER_BLOCKS_PER_STEP = 4
PROJ_ROWS = 256
VT_KEYS = PROJ_ROWS


def _params(sem):
    return pltpu.CompilerParams(dimension_semantics=sem, vmem_limit_bytes=VMEM_LIMIT)


def _const_spec(shape):
    nd = len(shape)
    return pl.BlockSpec(shape, lambda *_: (0,) * nd, pipeline_mode=pl.Buffered(1))


def _dot(a, b):
    return jnp.dot(a, b, preferred_element_type=F32)


def _dot_nt(a, b):
    return lax.dot_general(a, b, (((1,), (1,)), ((), ())), preferred_element_type=F32)


def _split3(x):
    hi = x.astype(BF16)
    r = x - hi.astype(F32)
    mid = r.astype(BF16)
    lo = (r - mid.astype(F32)).astype(BF16)
    return hi, mid, lo


def _dot3_lhs(x, w):
    hi, mid, lo = _split3(x)
    return _dot(hi, w) + _dot(mid, w) + _dot(lo, w)


def _dot3_rhs(w, x):
    hi, mid, lo = _split3(x)
    return _dot(w, hi) + _dot(w, mid) + _dot(w, lo)


def _rms(x, g):
    return x * lax.rsqrt(jnp.mean(x * x, axis=-1, keepdims=True) + EPS) * g


def _gelu(x):
    return 0.5 * x * (1.0 + lax.erf(x * SQRT_HALF))


def _silu(x):
    hx = 0.5 * x
    return hx + hx * jnp.tanh(hx)


def _tril(n):
    row = lax.broadcasted_iota(jnp.int32, (n, n), 0)
    col = lax.broadcasted_iota(jnp.int32, (n, n), 1)
    return col <= row


def _even_proj_body(x_ref, g_ref, wu_ref, wv_ref, wza_ref, wzb_ref, wx_ref, wdt_ref, gv_ref, cw_ref, cb_ref,
                    dtb_ref, conv0_ref, *refs, tm, seg, emit_v):
    gate_ref, vb_ref, zs_ref, xc_ref, dt_ref, cfin_ref, prev_sc = refs[:2] + refs[-5:]
    j = pl.program_id(1)
    nseg = tm // seg
    h = _rms(x_ref[...], g_ref[...]).astype(BF16)
    gate_ref[...] = _silu(_dot(h, wza_ref[...])) * _gelu(_dot(h, wu_ref[...]))
    gv = _gelu(_dot(h, wv_ref[...]))
    vn = jnp.concatenate(
        [_rms(gv[:, g * A_GROUP_DIM:(g + 1) * A_GROUP_DIM], gv_ref[g:g + 1, :]) for g in range(A_GROUPS)], axis=1)
    vb_ref[...] = vn.astype(BF16)
    if emit_v:
        refs[2][...] = vn
    zs_ref[...] = _silu(_dot(h, wzb_ref[...]))
    dt_ref[...] = jax.nn.softplus(_dot(h, wdt_ref[...]) + dtb_ref[...])

    tail = lambda s: slice(s * CTX_ROW + CTX_ROW - (B_CONV - 1), (s + 1) * CTX_ROW)

    @pl.when(j == 0)
    def _():
        prev_sc[...] = jnp.zeros_like(prev_sc)
        for s in range(nseg):
            prev_sc[tail(s), :] = conv0_ref[s]

    xbc = _dot(h, wx_ref[...])
    span = CTX_ROW + seg
    ext = jnp.concatenate([piece for s in range(nseg) for piece in
                           (prev_sc[s * CTX_ROW:(s + 1) * CTX_ROW, :], xbc[s * seg:(s + 1) * seg, :])], axis=0)
    acc = cb_ref[...] + xbc * cw_ref[B_CONV - 1:B_CONV, :]
    for back in range(1, B_CONV):
        rolled = pltpu.roll(ext, back, axis=0)
        shifted = jnp.concatenate([rolled[s * span + CTX_ROW:(s + 1) * span, :] for s in range(nseg)], axis=0)
        acc = acc + shifted * cw_ref[B_CONV - 1 - back:B_CONV - back, :]
    xc_ref[...] = _silu(acc)
    for s in range(nseg):
        prev_sc[s * CTX_ROW:(s + 1) * CTX_ROW, :] = xbc[(s + 1) * seg - CTX_ROW:(s + 1) * seg, :]

    @pl.when(j == pl.num_programs(1) - 1)
    def _():
        for s in range(nseg):
            cfin_ref[s] = prev_sc[tail(s), :]


def even_proj(x, g, ws, gv, cw, cb, dtb, conv0, tm, seg, emit_v):
    b, l, d = x.shape
    nseg = tm // seg
    assert nseg == 1 or l == tm
    tok = lambda w: pl.BlockSpec((None, tm, w), lambda i, j: (i, j, 0))
    ctx = pl.BlockSpec((None, nseg, B_CONV - 1, B_CONV_DIM), lambda i, j: (i, 0, 0, 0))
    consts = [g] + list(ws) + [gv, cw, cb, dtb]
    act = lambda w, dt: jax.ShapeDtypeStruct((b, l, w), dt)
    v_spec, v_shape = ([tok(A_WIDTH)], [act(A_WIDTH, F32)]) if emit_v else ([], [])
    return pl.pallas_call(
        functools.partial(_even_proj_body, tm=tm, seg=seg, emit_v=emit_v),
        grid=(b, l // tm),
        in_specs=[tok(d)] + [_const_spec(c.shape) for c in consts] + [ctx],
        out_specs=[tok(A_WIDTH), tok(A_WIDTH)] + v_spec + [tok(B_WIDTH), tok(B_CONV_DIM), tok(LANES), ctx],
        out_shape=[act(A_WIDTH, F32), act(A_WIDTH, BF16)] + v_shape
        + [act(B_WIDTH, F32), act(B_CONV_DIM, F32), act(LANES, F32),
           jax.ShapeDtypeStruct((b, nseg, B_CONV - 1, B_CONV_DIM), F32)],
        scratch_shapes=[pltpu.VMEM((nseg * CTX_ROW, B_CONV_DIM), F32)],
        compiler_params=_params(("parallel", "arbitrary")),
        name="even_proj",
    )(x, *consts, conv0)


def _out_proj_body(y_ref, w_ref, x_ref, g_ref, o_ref):
    o = _dot(y_ref[...], w_ref[...])
    o_ref[...] = x_ref[...] + _rms(o, g_ref[...])


def out_proj(y, w, x, g, tm):
    t, d = x.shape
    kd = y.shape[1]
    return pl.pallas_call(
        _out_proj_body,
        grid=(t // tm,),
        in_specs=[pl.BlockSpec((tm, kd), lambda i: (i, 0)), _const_spec(w.shape),
                  pl.BlockSpec((tm, d), lambda i: (i, 0)), _const_spec((1, d))],
        out_specs=pl.BlockSpec((tm, d), lambda i: (i, 0)),
        out_shape=jax.ShapeDtypeStruct((t, d), F32),
        compiler_params=_params(("parallel",)),
        name="out_proj",
    )(y, w, x, g)


def _even_mixer_body(gate_ref, vb_ref, zs_ref, xc_ref, dt_ref, ws_ref, bs_ref, alog_ref, dskip_ref, gssd_ref,
                     e_ref, s0_ref, y_ref, sfin_ref, st_sc, *, blk, nsub):
    j = pl.program_id(1)

    @pl.when(j == 0)
    def _():
        st_sc[...] = s0_ref[...].T

    for sub in range(nsub):
        _mixer_block(slice(sub * blk, (sub + 1) * blk), blk, gate_ref, vb_ref, zs_ref, xc_ref, dt_ref, ws_ref,
                     bs_ref, alog_ref, dskip_ref, gssd_ref, e_ref, y_ref, st_sc)

    @pl.when(j == pl.num_programs(1) - 1)
    def _():
        sfin_ref[...] = st_sc[...].T


def _mixer_block(rs, blk, gate_ref, vb_ref, zs_ref, xc_ref, dt_ref, ws_ref, bs_ref, alog_ref, dskip_ref, gssd_ref,
                 e_ref, y_ref, st_sc):
    row = lax.broadcasted_iota(jnp.int32, (blk, blk), 0)
    col = lax.broadcasted_iota(jnp.int32, (blk, blk), 1)
    causal = col <= row

    chunk_causal = (col // SUB_CHUNK) <= (row // SUB_CHUNK)
    s_parts = []
    for g in range(A_GROUPS):
        wg = jnp.where(chunk_causal, ws_ref[g], 0.0).astype(BF16)
        sg = _dot(wg, vb_ref[rs, g * A_GROUP_DIM:(g + 1) * A_GROUP_DIM])
        s_parts.append(sg + bs_ref[:, g:g + 1])
    ya = gate_ref[rs, :] * jnp.concatenate(s_parts, axis=1)

    xs = xc_ref[rs, :B_WIDTH]
    bm = xc_ref[rs, B_WIDTH:B_WIDTH + B_GROUPS * B_STATE]
    cm = xc_ref[rs, B_WIDTH + B_GROUPS * B_STATE:]
    lane = lax.broadcasted_iota(jnp.int32, (1, LANES), 1)
    dt = dt_ref[rs, :]
    a = jnp.where(lane < B_HEADS, -jnp.exp(alog_ref[...]), 0.0)
    acs = _dot3_rhs(causal.astype(BF16), dt * a)
    acs_t = acs.T
    e = e_ref[...]
    dt_e = _dot3_lhs(dt, e)
    acs_e = _dot3_lhs(acs, e)
    expacs_e = jnp.exp(acs_e)
    toend_e = jnp.exp(acs_e[blk - 1:blk, :] - acs_e)
    blkdec_e = expacs_e[blk - 1:blk, :]
    dtx = dt_e * xs
    wds = (toend_e * dtx).astype(BF16)
    dtx_b = dtx.astype(BF16)
    lane_hp = lax.broadcasted_iota(jnp.int32, (blk, LANES), 1)
    gw = (B_HEADS // B_GROUPS) * B_HEAD_DIM
    ys_parts, yi_parts = [], []
    for g in range(B_GROUPS):
        bm_g = bm[:, g * B_STATE:(g + 1) * B_STATE]
        cm_g = cm[:, g * B_STATE:(g + 1) * B_STATE].astype(BF16)
        cb = _dot_nt(cm_g, bm_g.astype(BF16))
        st_g = st_sc[:, g * gw:(g + 1) * gw]
        ds = _dot(bm_g.T.astype(BF16), wds[:, g * gw:(g + 1) * gw])
        st_sc[:, g * gw:(g + 1) * gw] = blkdec_e[:, g * gw:(g + 1) * gw] * st_g + ds
        ys_parts.append(_dot(cm_g, st_g.astype(BF16)))
        for hp in range(gw // LANES):
            c0 = g * gw + hp * LANES
            rhs = dtx_b[:, c0:c0 + LANES]
            pair = None
            for half in range(2):
                h = c0 // B_HEAD_DIM + half
                seg = acs[:, h:h + 1] - acs_t[h:h + 1, :]
                decay = jnp.exp(jnp.where(causal, seg, -jnp.inf))
                m_h = (cb * decay).astype(BF16)
                keep = (lane_hp < B_HEAD_DIM) if half == 0 else (lane_hp >= B_HEAD_DIM)
                part = _dot(m_h, jnp.where(keep, rhs, jnp.zeros_like(rhs)))
                pair = part if pair is None else pair + part
            yi_parts.append(pair)
    y = (jnp.concatenate(yi_parts, axis=1) + jnp.concatenate(ys_parts, axis=1) * expacs_e
         + dskip_ref[...] * xs)
    y = y * zs_ref[rs, :]
    half_w = B_WIDTH // B_GROUPS
    yb = jnp.concatenate(
        [_rms(y[:, g * half_w:(g + 1) * half_w], gssd_ref[:, g * half_w:(g + 1) * half_w])
         for g in range(B_GROUPS)], axis=1)
    y_ref[rs, :] = jnp.concatenate([ya, yb], axis=1).astype(y_ref.dtype)


def even_mixer(gate, vb, zs, xc, dt, ws, bs_t, alog, dskip_e, gssd, e_mat, s0, blk):
    b, l, _ = gate.shape
    nsub = MIXER_BLOCKS_PER_STEP if l % (MIXER_BLOCKS_PER_STEP * blk) == 0 else 1
    tok = lambda w: pl.BlockSpec((None, nsub * blk, w), lambda i, j: (i, j, 0))
    hp = B_HEADS * B_HEAD_DIM
    state = pl.BlockSpec((None, hp, B_STATE), lambda i, j: (i, 0, 0))
    consts = [ws, bs_t, alog, dskip_e, gssd, e_mat]
    return pl.pallas_call(
        functools.partial(_even_mixer_body, blk=blk, nsub=nsub),
        grid=(b, l // (nsub * blk)),
        in_specs=[tok(A_WIDTH), tok(A_WIDTH), tok(B_WIDTH), tok(B_CONV_DIM), tok(LANES)]
        + [_const_spec(c.shape) for c in consts] + [state],
        out_specs=[tok(A_WIDTH + B_WIDTH), state],
        out_shape=[jax.ShapeDtypeStruct((b, l, A_WIDTH + B_WIDTH), BF16), jax.ShapeDtypeStruct((b, hp, B_STATE), F32)],
        scratch_shapes=[pltpu.VMEM((B_STATE, hp), F32)],
        compiler_params=_params(("parallel", "arbitrary")),
        name="even_mixer",
    )(gate, vb, zs, xc, dt, *consts, s0)


def _forget_cumsum(lf, carry_sc, first):
    n = lf.shape[0]

    @pl.when(first)
    def _():
        carry_sc[...] = jnp.zeros_like(carry_sc)

    lane = lax.broadcasted_iota(jnp.int32, (1, LANES), 1)
    lf = jnp.where(lane < C_HEADS, lf, 0.0)
    chunk = _largest_tile(n, 256, 8)
    tri = _tril(chunk).astype(BF16)
    carry, parts = carry_sc[...], []
    for c in range(n // chunk):
        part = _dot3_rhs(tri, lf[c * chunk:(c + 1) * chunk, :]) + carry
        carry = part[chunk - 1:chunk, :]
        parts.append(part)
    carry_sc[...] = carry
    return jnp.concatenate(parts, axis=0)


def _forget_lanes(fc):
    lane = lax.broadcasted_iota(jnp.int32, (1, LANES), 1)
    hi, mid, lo = _split3(fc * LOG2E)
    pieces = (hi.astype(F32) + pltpu.roll(mid.astype(F32), C_HEADS, axis=1)
              + pltpu.roll(lo.astype(F32), 2 * C_HEADS, axis=1))
    q_side = pieces + ((lane >= AUG_HALF) & (lane < 2 * AUG_HALF)).astype(F32)
    k_side = (lane < AUG_HALF).astype(F32) - pltpu.roll(pieces, AUG_HALF, axis=1)
    return q_side.astype(BF16), k_side.astype(BF16)


def _head_lanes(k_side, head):
    lane = lax.broadcasted_iota(jnp.int32, (1, LANES), 1)
    return jnp.where((lane % C_HEADS == head) & (lane < 2 * AUG_HALF), k_side, jnp.zeros_like(k_side))


def _fox_proj_body(x_ref, g_ref, wq_ref, wk_ref, wv_ref, wz_ref, wf_ref, gq_ref, gk_ref, bf_ref, *refs,
                   slot, first, with_aug):
    outs = refs[2:] if not first else refs
    q_ref, kf_ref, kb_ref, vf_ref, vb_ref, z_ref, lf_ref = outs[:7]
    tm = x_ref.shape[0]
    if first:
        kf_ref[...] = jnp.zeros(kf_ref.shape, F32)
        vf_ref[...] = jnp.zeros(vf_ref.shape, F32)
        kf_ref, vf_ref = kf_ref.at[slot], vf_ref.at[slot]
    h = _rms(x_ref[...], g_ref[...]).astype(BF16)
    q = _dot(h, wq_ref[...])
    k = _dot(h, wk_ref[...])
    v = _dot(h, wv_ref[...])
    q_scale = C_HEAD_DIM ** -0.5 * LOG2E
    kn = []
    for hd in range(C_HEADS):
        sl = slice(hd * C_HEAD_DIM, (hd + 1) * C_HEAD_DIM)
        q_ref[:, sl] = (_rms(q[:, sl], gq_ref[...]) * q_scale).astype(BF16)
        kn.append(_rms(k[:, sl], gk_ref[...]))
    kn = jnp.concatenate(kn, axis=1)
    kf_ref[...] = kn.reshape(tm, C_HEADS, C_HEAD_DIM)
    kb_ref[...] = kn.astype(BF16)
    vf_ref[...] = v.reshape(tm, C_HEADS, C_HEAD_DIM)
    vb_ref[...] = (v.T if with_aug else v).astype(BF16)
    z_ref[...] = _dot(h, wz_ref[...])
    lf = jax.nn.log_sigmoid(_dot(h, wf_ref[...]) + bf_ref[...])
    lf_ref[...] = lf
    if with_aug:
        qa_ref, ka_ref, carry_sc = outs[7:]
        qa_ref[...], ka_ref[...] = _forget_lanes(_forget_cumsum(lf, carry_sc, pl.program_id(1) == 0))


def fox_proj(x, g, wq, wk, wv, wz, wf, gq, gk, bf, tm, n_slots, slot, prev_k, prev_v, with_aug):
    b, l, d = x.shape
    tok = lambda w: pl.BlockSpec((None, tm, w), lambda i, j: (i, j, 0))
    act = lambda w, dt: jax.ShapeDtypeStruct((b, l, w), dt)
    ins = [g, wq, wk, wv, wz, wf, gq, gk, bf]
    first = prev_k is None
    stacked = jax.ShapeDtypeStruct((n_slots, b, l, C_HEADS, C_HEAD_DIM), F32)
    if first:
        st_spec = pl.BlockSpec((n_slots, None, tm, C_HEADS, C_HEAD_DIM), lambda i, j: (0, i, j, 0, 0))
        extra, extra_specs, aliases = [], [], {}
    else:
        st_spec = pl.BlockSpec((None, None, tm, C_HEADS, C_HEAD_DIM), lambda i, j: (slot, i, j, 0, 0))
        extra, extra_specs = [prev_k, prev_v], [pl.BlockSpec(memory_space=pl.ANY)] * 2
        aliases = {1 + len(ins): 1, 2 + len(ins): 3}
    w = C_HEADS * C_HEAD_DIM
    aug_specs, aug_shapes, scratch = [], [], []
    v_spec, v_shape = tok(w), act(w, BF16)
    if with_aug:
        assert tm == VT_KEYS
        aug_specs, aug_shapes = [tok(LANES)] * 2, [act(LANES, BF16)] * 2
        scratch = [pltpu.VMEM((1, LANES), F32)]
        v_spec = pl.BlockSpec((None, None, w, tm), lambda i, j: (i, j, 0, 0))
        v_shape = jax.ShapeDtypeStruct((b, l // tm, w, tm), BF16)
    return pl.pallas_call(
        functools.partial(_fox_proj_body, slot=slot, first=first, with_aug=with_aug),
        grid=(b, l // tm),
        in_specs=[tok(d)] + [_const_spec(a.shape) for a in ins] + extra_specs,
        out_specs=[tok(w), st_spec, tok(w), st_spec, v_spec, tok(w), tok(LANES)] + aug_specs,
        out_shape=[act(w, BF16), stacked, act(w, BF16), stacked, v_shape, act(w, F32), act(LANES, F32)]
        + aug_shapes,
        input_output_aliases=aliases,
        scratch_shapes=scratch,
        compiler_params=_params(("parallel", "arbitrary")),
        name="fox_proj",
    )(x, *ins, *extra)


def _fox_aug_body(lf_ref, qa_ref, ka_ref, carry_sc):
    qa_ref[...], ka_ref[...] = _forget_lanes(_forget_cumsum(lf_ref[...], carry_sc, pl.program_id(1) == 0))


def fox_aug(lf, ts):
    b, s, _ = lf.shape
    return pl.pallas_call(
        _fox_aug_body,
        grid=(b, s // ts),
        in_specs=[pl.BlockSpec((None, ts, LANES), lambda i, j: (i, j, 0))],
        out_specs=[pl.BlockSpec((None, ts, LANES), lambda i, j: (i, j, 0))] * 2,
        out_shape=[jax.ShapeDtypeStruct((b, s, LANES), BF16)] * 2,
        scratch_shapes=[pltpu.VMEM((1, LANES), F32)],
        compiler_params=_params(("parallel", "arbitrary")),
        name="fox_aug",
    )(lf)


ATTN_ROWS = 1024
ATTN_CHAINS = 2


def _softmax_step(qc, carry, kc, vts, diagonal):
    m, l, acc = carry
    st = _dot_nt(kc, qc)
    if diagonal:
        row = lax.broadcasted_iota(jnp.int32, st.shape, 0)
        col = lax.broadcasted_iota(jnp.int32, st.shape, 1)
        st = jnp.where(row <= col, st, -jnp.inf)
    m_new = jnp.maximum(m, jnp.max(st, axis=0, keepdims=True))
    alpha = jnp.exp2(m - m_new)
    p = jnp.exp2(st - m_new)
    l = alpha * l + jnp.sum(p, axis=0, keepdims=True)
    pb = p.astype(BF16)
    pv = _dot(vts[0], pb[:VT_KEYS, :])
    for u in range(1, len(vts)):
        pv = pv + _dot(vts[u], pb[u * VT_KEYS:(u + 1) * VT_KEYS, :])
    return m_new, l, alpha * acc + pv


def _fox_attn_body(q_ref, qa_ref, k_ref, ka_ref, vt_ref, z_ref, o_ref, *, rows, chains, part):
    qi = pl.program_id(2)
    tq = rows * chains
    tiles = lambda n: n // VT_KEYS

    def queries(r0, n):
        return jnp.concatenate([q_ref[r0:r0 + n, :], qa_ref[r0:r0 + n, :]], axis=1)

    def keys(r0, n):
        return jnp.concatenate([k_ref[pl.ds(r0, n), :], _head_lanes(ka_ref[pl.ds(r0, n), :], pl.program_id(1))],
                               axis=1)

    qcs = [queries(c * rows, rows) for c in range(chains)]

    def below_diagonal(j, carries):
        kc = keys(pl.multiple_of(j * tq, tq), tq)
        vts = [vt_ref[j * tiles(tq) + u] for u in range(tiles(tq))]
        return tuple(_softmax_step(qcs[c], carries[c], kc, vts, False) for c in range(chains))

    init = tuple((jnp.full((1, rows), -jnp.inf, F32), jnp.zeros((1, rows), F32), jnp.zeros((C_HEAD_DIM, rows), F32))
                 for _ in range(chains))
    carries = lax.fori_loop(0, qi, below_diagonal, init)

    base = pl.multiple_of(qi * tq, tq)
    kc = keys(base, tq)
    vts = [vt_ref[qi * tiles(tq) + u] for u in range(tiles(tq))]
    for i in range(tq // part):
        c, o = divmod(i * part, rows)
        carry = tuple(a[:, o:o + part] for a in carries[c])
        qc = queries(i * part, part)
        if i > 0:
            carry = _softmax_step(qc, carry, kc[:i * part, :], vts[:tiles(i * part)], False)
        _, l, acc = _softmax_step(qc, carry, kc[i * part:(i + 1) * part, :],
                                  vts[tiles(i * part):tiles((i + 1) * part)], True)
        sl = slice(i * part, (i + 1) * part)
        o_ref[sl, :] = (_silu(z_ref[sl, :]) * (acc / l).T).astype(o_ref.dtype)


def fox_attn(q, qa, k, ka, vt, z):
    b, l, w = q.shape
    chains = ATTN_CHAINS if l % (ATTN_ROWS * ATTN_CHAINS) == 0 else 1
    rows = ATTN_ROWS if l % (ATTN_ROWS * chains) == 0 else l // chains
    tq = rows * chains
    part = rows
    qspec = pl.BlockSpec((None, tq, C_HEAD_DIM), lambda i, h, j: (i, j, h))
    qaspec = pl.BlockSpec((None, tq, LANES), lambda i, h, j: (i, j, 0))
    kspec = pl.BlockSpec((None, l, C_HEAD_DIM), lambda i, h, j: (i, 0, h))
    kaspec = pl.BlockSpec((None, l, LANES), lambda i, h, j: (i, 0, 0))
    vspec = pl.BlockSpec((None, l // VT_KEYS, C_HEAD_DIM, VT_KEYS), lambda i, h, j: (i, 0, h, 0))
    return pl.pallas_call(
        functools.partial(_fox_attn_body, rows=rows, chains=chains, part=part),
        grid=(b, C_HEADS, l // tq),
        in_specs=[qspec, qaspec, kspec, kaspec, vspec, qspec],
        out_specs=qspec,
        out_shape=jax.ShapeDtypeStruct((b, l, w), BF16),
        compiler_params=_params(("parallel", "parallel", "arbitrary")),
        name="fox_attn",
    )(q, qa, k, ka, vt, z)


def _fox_attn_sample_body(q_ref, qa_ref, kp_ref, kap_ref, vp_ref, kn_ref, kan_ref, vn_ref, z_ref, o_ref, *, p0):
    n = q_ref.shape[0]
    for hd in range(C_HEADS):
        sl = slice(hd * C_HEAD_DIM, (hd + 1) * C_HEAD_DIM)
        head_rows = pl.ds(hd, p0, stride=C_HEADS)
        qc = jnp.concatenate([q_ref[:, sl], qa_ref[...]], axis=1)
        s_past = _dot_nt(qc, jnp.concatenate([kp_ref[head_rows, :].astype(BF16), _head_lanes(kap_ref[...], hd)],
                                             axis=1))
        s_new = _dot_nt(qc, jnp.concatenate([kn_ref[:, sl], _head_lanes(kan_ref[...], hd)], axis=1))
        s_new = jnp.where(_tril(n), s_new, -jnp.inf)
        m = jnp.maximum(jnp.max(s_past, axis=-1, keepdims=True), jnp.max(s_new, axis=-1, keepdims=True))
        p_past = jnp.exp2(s_past - m)
        p_new = jnp.exp2(s_new - m)
        l = jnp.sum(p_past, axis=-1, keepdims=True) + jnp.sum(p_new, axis=-1, keepdims=True)
        acc = (_dot(p_past.astype(BF16), vp_ref[head_rows, :].astype(BF16))
               + _dot(p_new.astype(BF16), vn_ref[:, sl]))
        o_ref[:, sl] = (_silu(z_ref[:, sl]) * (acc / l)).astype(o_ref.dtype)


def fox_attn_sample(q, qa_all, cache_k, ka_all, cache_v, layer, k_new, v_new, z):
    b, n, w = q.shape
    p0 = cache_k.shape[2] // C_HEADS
    new = pl.BlockSpec((None, n, w), lambda i: (i, 0, 0))
    new_of_all = pl.BlockSpec((None, n, LANES), lambda i: (i, p0 // n, 0))
    past_aug = pl.BlockSpec((None, p0, LANES), lambda i: (i, 0, 0))
    past = pl.BlockSpec((None, None, p0 * C_HEADS, C_HEAD_DIM), lambda i: (layer, i, 0, 0))
    return pl.pallas_call(
        functools.partial(_fox_attn_sample_body, p0=p0),
        grid=(b,),
        in_specs=[new, new_of_all, past, past_aug, past, new, new_of_all, new, new],
        out_specs=new,
        out_shape=jax.ShapeDtypeStruct((b, n, w), BF16),
        compiler_params=_params(("parallel",)),
        name="fox_attn_sample",
    )(q, qa_all, cache_k, ka_all, cache_v, k_new, ka_all, v_new, z)


def _largest_tile(n, cap, mult):
    t = min(n, cap)
    while n % t or t % mult:
        t -= 1
    return t


def _pad_cols(a, width):
    return jnp.pad(a, ((0, 0), (0, width - a.shape[1])))


def _even_weights(w_in, w_out, ws, bs, gv, conv_w, conv_b, dt_bias, a_log, d_skip, g_ssd):
    cuts = (A_WIDTH, 2 * A_WIDTH, 3 * A_WIDTH, 3 * A_WIDTH + B_WIDTH, 3 * A_WIDTH + B_WIDTH + B_CONV_DIM)
    wu, wv, wza, wzb, wx, wdt = jnp.split(w_in, cuts, axis=1)
    e_mat = np.zeros((LANES, B_HEADS * B_HEAD_DIM), np.float32)
    for h in range(B_HEADS):
        e_mat[h, h * B_HEAD_DIM:(h + 1) * B_HEAD_DIM] = 1.0
    return dict(
        w_in=[p.astype(BF16) for p in (wu, wv, wza, wzb, wx, _pad_cols(wdt, LANES))], w_out=w_out.astype(BF16),
        ws=ws, bs=bs, gv=gv, cw=conv_w.T, cb=conv_b[None, :], dtb=_pad_cols(dt_bias[None, :], LANES),
        alog=_pad_cols(a_log[None, :], LANES), dskip_e=jnp.repeat(d_skip, B_HEAD_DIM)[None, :],
        gssd=g_ssd[None, :], e_mat=jnp.asarray(e_mat, BF16))


def _even_layer(x, g_pre, g_post, wts, conv0, s0, emit_v):
    b, l, d = x.shape
    t = b * l
    if l >= PROJ_ROWS:
        xin, tm, seg = x, _largest_tile(l, PROJ_ROWS, 16), _largest_tile(l, PROJ_ROWS, 16)
    else:
        xin, tm, seg = x.reshape(1, t, d), t, l
    outs = even_proj(xin, g_pre[None, :], wts["w_in"], wts["gv"], wts["cw"], wts["cb"], wts["dtb"],
                     conv0.reshape(xin.shape[0], tm // seg, B_CONV - 1, B_CONV_DIM), tm, seg, emit_v)
    gate, vb, *vrows, zs, xc, dt = (a.reshape(b, l, a.shape[2]) for a in outs[:-1])
    cfin = outs[-1].reshape(b, B_CONV - 1, B_CONV_DIM)
    blk = min(l, A_CHUNK)
    ycat, sfin = even_mixer(gate, vb, zs, xc, dt, wts["ws"][:, :blk, :blk], wts["bs"][:, :blk].T, wts["alog"],
                            wts["dskip_e"], wts["gssd"], wts["e_mat"],
                            s0.reshape(b, B_HEADS * B_HEAD_DIM, B_STATE), blk)
    xn = out_proj(ycat.reshape(t, ycat.shape[2]), wts["w_out"], x.reshape(t, d), g_post[None, :],
                  _largest_tile(t, 512, 8))
    return (xn.reshape(b, l, d), *vrows, sfin.reshape(b, B_HEADS, B_HEAD_DIM, B_STATE), cfin)


def _odd_weights(w_in, b_f, w_out, gq, gk):
    w = C_HEADS * C_HEAD_DIM
    wq, wk, wv, wz, wf = jnp.split(w_in, (w, 2 * w, 3 * w, 4 * w), axis=1)
    return dict(wq=wq.astype(BF16), wk=wk.astype(BF16), wv=wv.astype(BF16), wz=wz.astype(BF16),
                wf=_pad_cols(wf, LANES).astype(BF16), bf=_pad_cols(b_f[None, :], LANES),
                w_out=w_out.astype(BF16), gq=gq[None, :], gk=gk[None, :])


def _odd_layer(x, g_pre, g_post, wts, n_slots, slot, prev_k, prev_v, cache):
    b, l, d = x.shape
    t = b * l
    prompt = cache is None
    xin = x if prompt else x.reshape(1, t, d)
    q, kst, kb, vst, vb, z, lf, *aug = fox_proj(
        xin, g_pre[None, :], wts["wq"], wts["wk"], wts["wv"], wts["wz"], wts["wf"], wts["gq"], wts["gk"],
        wts["bf"], _largest_tile(xin.shape[1], PROJ_ROWS, 16), n_slots, slot, prev_k, prev_v, prompt)
    if prompt:
        yc = fox_attn(q, aug[0], kb, aug[1], vb, z)
    else:
        q, kb, vb, z, lf = (a.reshape(b, l, a.shape[2]) for a in (q, kb, vb, z, lf))
        cache_k, cache_v, cache_logf = cache
        p0 = cache_k.shape[2]
        past_lf = _pad_cols(cache_logf[slot].reshape(b * p0, C_HEADS), LANES).reshape(b, p0, LANES)
        qa, ka = fox_aug(jnp.concatenate([past_lf, lf], axis=1), _largest_tile(p0 + l, 1040, 16))
        rows = lambda c: c.reshape(c.shape[0], b, p0 * C_HEADS, C_HEAD_DIM)
        yc = fox_attn_sample(q, qa, rows(cache_k), ka, rows(cache_v), slot, kb, vb, z)
    xn = out_proj(yc.reshape(t, d), wts["w_out"], x.reshape(t, d), g_post[None, :], _largest_tile(t, 512, 8))
    return xn.reshape(b, l, d), kst, vst, lf[:, :, :C_HEADS]


def kernel(x_prompt, x_sample, cache_fox_k, cache_fox_v, cache_fox_logf, state_ssd, state_conv, norm_pre, norm_post, w_in_even, w_out_even, gmlp_ws, gmlp_bs, gmlp_gv, ssd_conv_w, ssd_conv_b, ssd_dt_bias, ssd_a_log, ssd_d, ssd_norm_g, w_in_odd, fox_b_forget, w_out_odd, fox_gq, fox_gk):
    xp, xs = x_prompt, x_sample
    bp = xp.shape[0]
    depth = norm_pre.shape[0]
    n_odd = depth // 2
    outs = {n: [] for n in ("lp", "sp", "cp", "ls", "ss", "cs", "gs")}
    kp = vp = ks = vs = None
    cache = (cache_fox_k, cache_fox_v, cache_fox_logf)
    for i in range(depth):
        j = i // 2
        if i % 2 == 0:
            wts = _even_weights(w_in_even[j], w_out_even[j], gmlp_ws[j], gmlp_bs[j], gmlp_gv[j], ssd_conv_w[j],
                                ssd_conv_b[j], ssd_dt_bias[j], ssd_a_log[j], ssd_d[j], ssd_norm_g[j])
            xp, sp, cp = _even_layer(xp, norm_pre[i], norm_post[i], wts,
                                     jnp.zeros((bp, B_CONV - 1, B_CONV_DIM), F32),
                                     jnp.zeros((bp, B_HEADS, B_HEAD_DIM, B_STATE), F32), False)
            xs, gs, ss, cs = _even_layer(xs, norm_pre[i], norm_post[i], wts, state_conv[j], state_ssd[j], True)
            for n, a in (("sp", sp), ("cp", cp), ("ss", ss), ("cs", cs), ("gs", gs)):
                outs[n].append(a)
        else:
            wts = _odd_weights(w_in_odd[j], fox_b_forget[j], w_out_odd[j], fox_gq[j], fox_gk[j])
            xp, kp, vp, lp = _odd_layer(xp, norm_pre[i], norm_post[i], wts, n_odd, j, kp, vp, None)
            xs, ks, vs, ls = _odd_layer(xs, norm_pre[i], norm_post[i], wts, n_odd, j, ks, vs, cache)
            outs["lp"].append(lp)
            outs["ls"].append(ls)
    st = {n: jnp.stack(a) for n, a in outs.items()}
    ks, vs = (a.reshape((n_odd,) + xs.shape[:2] + (C_HEADS, C_HEAD_DIM)) for a in (ks, vs))
    return (xp, xs, kp, vp, st["lp"], st["sp"], st["cp"], ks, vs, st["ls"], st["ss"], st["cs"], st["gs"])
```

```python
import functools

import numpy as np
import jax
import jax.numpy as jnp
from jax import lax
from jax.experimental import pallas as pl
from jax.experimental.pallas import tpu as pltpu

F32 = jnp.float32
BF16 = jnp.bfloat16
EPS = 1e-6
LANES = 128
VMEM_LIMIT = 56 * 1024 * 1024

D_MODEL = 1024
A_GROUPS = 4
A_GROUP_DIM = 256
A_WIDTH = A_GROUPS * A_GROUP_DIM
A_CHUNK = 128
SUB_CHUNK = 64
B_HEADS = 16
B_HEAD_DIM = 64
B_GROUPS = 2
B_STATE = 128
B_CONV = 4
B_WIDTH = 1024
B_CONV_DIM = B_WIDTH + 2 * B_GROUPS * B_STATE
CTX_ROW = 8
C_HEADS = 8
C_HEAD_DIM = 128
SQRT_HALF = 0.7071067811865476
LOG2E = 1.4426950408889634
AUG_PIECES = 3
AUG_HALF = AUG_PIECES * C_HEADS
MIXER_BLOCKS_PER_STEP = 4
PROJ_ROWS = 256
VT_KEYS = PROJ_ROWS


def _params(sem):
    return pltpu.CompilerParams(dimension_semantics=sem, vmem_limit_bytes=VMEM_LIMIT)


def _const_spec(shape):
    nd = len(shape)
    return pl.BlockSpec(shape, lambda *_: (0,) * nd, pipeline_mode=pl.Buffered(1))


def _dot(a, b):
    return jnp.dot(a, b, preferred_element_type=F32)


def _dot_nt(a, b):
    return lax.dot_general(a, b, (((1,), (1,)), ((), ())), preferred_element_type=F32)


def _split3(x):
    hi = x.astype(BF16)
    r = x - hi.astype(F32)
    mid = r.astype(BF16)
    lo = (r - mid.astype(F32)).astype(BF16)
    return hi, mid, lo


def _dot3_lhs(x, w):
    hi, mid, lo = _split3(x)
    return _dot(hi, w) + _dot(mid, w) + _dot(lo, w)


def _dot3_rhs(w, x):
    hi, mid, lo = _split3(x)
    return _dot(w, hi) + _dot(w, mid) + _dot(w, lo)


def _rms(x, g):
    return x * lax.rsqrt(jnp.mean(x * x, axis=-1, keepdims=True) + EPS) * g


def _gelu(x):
    return 0.5 * x * (1.0 + lax.erf(x * SQRT_HALF))


def _silu(x):
    hx = 0.5 * x
    return hx + hx * jnp.tanh(hx)


def _tril(n):
    row = lax.broadcasted_iota(jnp.int32, (n, n), 0)
    col = lax.broadcasted_iota(jnp.int32, (n, n), 1)
    return col <= row


def _even_proj_body(x_ref, g_ref, wu_ref, wv_ref, wza_ref, wzb_ref, wx_ref, wdt_ref, gv_ref, cw_ref, cb_ref,
                    dtb_ref, conv0_ref, *refs, tm, seg, emit_v):
    gate_ref, vb_ref, zs_ref, xc_ref, dt_ref, cfin_ref, prev_sc = refs[:2] + refs[-5:]
    j = pl.program_id(1)
    nseg = tm // seg
    tail = lambda s: slice(s * CTX_ROW + CTX_ROW - (B_CONV - 1), (s + 1) * CTX_ROW)

    @pl.when(j == 0)
    def _():
        prev_sc[...] = jnp.zeros_like(prev_sc)
        for s in range(nseg):
            prev_sc[tail(s), :] = conv0_ref[s]

    h = _rms(x_ref[...], g_ref[...]).astype(BF16)
    xbc = _dot(h, wx_ref[...])
    span = CTX_ROW + seg
    ext = jnp.concatenate([piece for s in range(nseg) for piece in
                           (prev_sc[s * CTX_ROW:(s + 1) * CTX_ROW, :], xbc[s * seg:(s + 1) * seg, :])], axis=0)
    acc = cb_ref[...] + xbc * cw_ref[B_CONV - 1:B_CONV, :]
    for back in range(1, B_CONV):
        rolled = pltpu.roll(ext, back, axis=0)
        shifted = jnp.concatenate([rolled[s * span + CTX_ROW:(s + 1) * span, :] for s in range(nseg)], axis=0)
        acc = acc + shifted * cw_ref[B_CONV - 1 - back:B_CONV - back, :]
    xc_ref[...] = _silu(acc)
    for s in range(nseg):
        prev_sc[s * CTX_ROW:(s + 1) * CTX_ROW, :] = xbc[(s + 1) * seg - CTX_ROW:(s + 1) * seg, :]
    gv = _gelu(_dot(h, wv_ref[...]))
    vn = jnp.concatenate(
        [_rms(gv[:, g * A_GROUP_DIM:(g + 1) * A_GROUP_DIM], gv_ref[g:g + 1, :]) for g in range(A_GROUPS)], axis=1)
    vb_ref[...] = vn.astype(BF16)
    if emit_v:
        refs[2][...] = vn
    gate_ref[...] = _silu(_dot(h, wza_ref[...])) * _gelu(_dot(h, wu_ref[...]))
    zs_ref[...] = _silu(_dot(h, wzb_ref[...]))
    dt_ref[...] = jax.nn.softplus(_dot(h, wdt_ref[...]) + dtb_ref[...])

    @pl.when(j == pl.num_programs(1) - 1)
    def _():
        for s in range(nseg):
            cfin_ref[s] = prev_sc[tail(s), :]


def even_proj(x, g, ws, gv, cw, cb, dtb, conv0, tm, seg, emit_v):
    b, l, d = x.shape
    nseg = tm // seg
    assert nseg == 1 or l == tm
    tok = lambda w: pl.BlockSpec((None, tm, w), lambda i, j: (i, j, 0))
    ctx = pl.BlockSpec((None, nseg, B_CONV - 1, B_CONV_DIM), lambda i, j: (i, 0, 0, 0))
    consts = [g] + list(ws) + [gv, cw, cb, dtb]
    act = lambda w, dt: jax.ShapeDtypeStruct((b, l, w), dt)
    v_spec, v_shape = ([tok(A_WIDTH)], [act(A_WIDTH, F32)]) if emit_v else ([], [])
    return pl.pallas_call(
        functools.partial(_even_proj_body, tm=tm, seg=seg, emit_v=emit_v),
        grid=(b, l // tm),
        in_specs=[tok(d)] + [_const_spec(c.shape) for c in consts] + [ctx],
        out_specs=[tok(A_WIDTH), tok(A_WIDTH)] + v_spec + [tok(B_WIDTH), tok(B_CONV_DIM), tok(LANES), ctx],
        out_shape=[act(A_WIDTH, F32), act(A_WIDTH, BF16)] + v_shape
        + [act(B_WIDTH, F32), act(B_CONV_DIM, F32), act(LANES, F32),
           jax.ShapeDtypeStruct((b, nseg, B_CONV - 1, B_CONV_DIM), F32)],
        scratch_shapes=[pltpu.VMEM((nseg * CTX_ROW, B_CONV_DIM), F32)],
        compiler_params=_params(("parallel", "arbitrary")),
        name="even_proj",
    )(x, *consts, conv0)


def _out_proj_body(y_ref, w_ref, x_ref, g_ref, o_ref):
    o = _dot(y_ref[...], w_ref[...])
    o_ref[...] = x_ref[...] + _rms(o, g_ref[...])


def out_proj(y, w, x, g, tm):
    t, d = x.shape
    kd = y.shape[1]
    return pl.pallas_call(
        _out_proj_body,
        grid=(t // tm,),
        in_specs=[pl.BlockSpec((tm, kd), lambda i: (i, 0)), _const_spec(w.shape),
                  pl.BlockSpec((tm, d), lambda i: (i, 0)), _const_spec((1, d))],
        out_specs=pl.BlockSpec((tm, d), lambda i: (i, 0)),
        out_shape=jax.ShapeDtypeStruct((t, d), F32),
        compiler_params=_params(("parallel",)),
        name="out_proj",
    )(y, w, x, g)


def _even_mixer_body(gate_ref, vb_ref, zs_ref, xc_ref, dt_ref, ws_ref, bs_ref, alog_ref, dskip_ref, gssd_ref,
                     e_ref, s0_ref, y_ref, sfin_ref, st_sc, *, blk, nsub):
    j = pl.program_id(1)

    @pl.when(j == 0)
    def _():
        st_sc[...] = s0_ref[...].T

    for sub in range(nsub):
        _mixer_block(slice(sub * blk, (sub + 1) * blk), blk, gate_ref, vb_ref, zs_ref, xc_ref, dt_ref, ws_ref,
                     bs_ref, alog_ref, dskip_ref, gssd_ref, e_ref, y_ref, st_sc)

    @pl.when(j == pl.num_programs(1) - 1)
    def _():
        sfin_ref[...] = st_sc[...].T


def _mixer_block(rs, blk, gate_ref, vb_ref, zs_ref, xc_ref, dt_ref, ws_ref, bs_ref, alog_ref, dskip_ref, gssd_ref,
                 e_ref, y_ref, st_sc):
    row = lax.broadcasted_iota(jnp.int32, (blk, blk), 0)
    col = lax.broadcasted_iota(jnp.int32, (blk, blk), 1)
    causal = col <= row

    chunk_causal = (col // SUB_CHUNK) <= (row // SUB_CHUNK)
    s_parts = []
    for g in range(A_GROUPS):
        wg = jnp.where(chunk_causal, ws_ref[g], 0.0).astype(BF16)
        sg = _dot(wg, vb_ref[rs, g * A_GROUP_DIM:(g + 1) * A_GROUP_DIM])
        s_parts.append(sg + bs_ref[:, g:g + 1])
    ya = gate_ref[rs, :] * jnp.concatenate(s_parts, axis=1)

    xs = xc_ref[rs, :B_WIDTH]
    bm = xc_ref[rs, B_WIDTH:B_WIDTH + B_GROUPS * B_STATE]
    cm = xc_ref[rs, B_WIDTH + B_GROUPS * B_STATE:]
    lane = lax.broadcasted_iota(jnp.int32, (1, LANES), 1)
    dt = dt_ref[rs, :]
    a = jnp.where(lane < B_HEADS, -jnp.exp(alog_ref[...]), 0.0)
    acs = _dot3_rhs(causal.astype(BF16), dt * a)
    acs_t = acs.T
    e = e_ref[...]
    dt_e = _dot3_lhs(dt, e)
    acs_e = _dot3_lhs(acs, e)
    expacs_e = jnp.exp(acs_e)
    toend_e = jnp.exp(acs_e[blk - 1:blk, :] - acs_e)
    blkdec_e = expacs_e[blk - 1:blk, :]
    dtx = dt_e * xs
    wds = (toend_e * dtx).astype(BF16)
    dtx_b = dtx.astype(BF16)
    lane_hp = lax.broadcasted_iota(jnp.int32, (blk, LANES), 1)
    gw = (B_HEADS // B_GROUPS) * B_HEAD_DIM
    ys_parts, yi_parts = [], []
    for g in range(B_GROUPS):
        bm_g = bm[:, g * B_STATE:(g + 1) * B_STATE]
        cm_g = cm[:, g * B_STATE:(g + 1) * B_STATE].astype(BF16)
        cb = _dot_nt(cm_g, bm_g.astype(BF16))
        st_g = st_sc[:, g * gw:(g + 1) * gw]
        ds = _dot(bm_g.T.astype(BF16), wds[:, g * gw:(g + 1) * gw])
        st_sc[:, g * gw:(g + 1) * gw] = blkdec_e[:, g * gw:(g + 1) * gw] * st_g + ds
        ys_parts.append(_dot(cm_g, st_g.astype(BF16)))
        for hp in range(gw // LANES):
            c0 = g * gw + hp * LANES
            rhs = dtx_b[:, c0:c0 + LANES]
            pair = None
            for half in range(2):
                h = c0 // B_HEAD_DIM + half
                seg = acs[:, h:h + 1] - acs_t[h:h + 1, :]
                decay = jnp.exp(jnp.where(causal, seg, -jnp.inf))
                m_h = (cb * decay).astype(BF16)
                keep = (lane_hp < B_HEAD_DIM) if half == 0 else (lane_hp >= B_HEAD_DIM)
                part = _dot(m_h, jnp.where(keep, rhs, jnp.zeros_like(rhs)))
                pair = part if pair is None else pair + part
            yi_parts.append(pair)
    y = (jnp.concatenate(yi_parts, axis=1) + jnp.concatenate(ys_parts, axis=1) * expacs_e
         + dskip_ref[...] * xs)
    y = y * zs_ref[rs, :]
    half_w = B_WIDTH // B_GROUPS
    yb = jnp.concatenate(
        [_rms(y[:, g * half_w:(g + 1) * half_w], gssd_ref[:, g * half_w:(g + 1) * half_w])
         for g in range(B_GROUPS)], axis=1)
    y_ref[rs, :] = jnp.concatenate([ya, yb], axis=1).astype(y_ref.dtype)


def even_mixer(gate, vb, zs, xc, dt, ws, bs_t, alog, dskip_e, gssd, e_mat, s0, blk):
    b, l, _ = gate.shape
    nsub = MIXER_BLOCKS_PER_STEP if l % (MIXER_BLOCKS_PER_STEP * blk) == 0 else 1
    tok = lambda w: pl.BlockSpec((None, nsub * blk, w), lambda i, j: (i, j, 0))
    hp = B_HEADS * B_HEAD_DIM
    state = pl.BlockSpec((None, hp, B_STATE), lambda i, j: (i, 0, 0))
    consts = [ws, bs_t, alog, dskip_e, gssd, e_mat]
    return pl.pallas_call(
        functools.partial(_even_mixer_body, blk=blk, nsub=nsub),
        grid=(b, l // (nsub * blk)),
        in_specs=[tok(A_WIDTH), tok(A_WIDTH), tok(B_WIDTH), tok(B_CONV_DIM), tok(LANES)]
        + [_const_spec(c.shape) for c in consts] + [state],
        out_specs=[tok(A_WIDTH + B_WIDTH), state],
        out_shape=[jax.ShapeDtypeStruct((b, l, A_WIDTH + B_WIDTH), BF16), jax.ShapeDtypeStruct((b, hp, B_STATE), F32)],
        scratch_shapes=[pltpu.VMEM((B_STATE, hp), F32)],
        compiler_params=_params(("parallel", "arbitrary")),
        name="even_mixer",
    )(gate, vb, zs, xc, dt, *consts, s0)


def _reset_at_stream_start(carry_sc):
    @pl.when(pl.program_id(1) == 0)
    def _():
        carry_sc[...] = jnp.zeros_like(carry_sc)


def _forget_cumsum(lf, carry_sc):
    n = lf.shape[0]
    lane = lax.broadcasted_iota(jnp.int32, (1, LANES), 1)
    lf = jnp.where(lane < C_HEADS, lf, 0.0)
    chunk = _largest_tile(n, 256, 8)
    tri = _tril(chunk).astype(BF16)
    carry, parts = carry_sc[...], []
    for c in range(n // chunk):
        part = _dot3_rhs(tri, lf[c * chunk:(c + 1) * chunk, :]) + carry
        carry = part[chunk - 1:chunk, :]
        parts.append(part)
    carry_sc[...] = carry
    return jnp.concatenate(parts, axis=0)


def _forget_lanes(fc):
    lane = lax.broadcasted_iota(jnp.int32, (1, LANES), 1)
    hi, mid, lo = _split3(fc * LOG2E)
    pieces = (hi.astype(F32) + pltpu.roll(mid.astype(F32), C_HEADS, axis=1)
              + pltpu.roll(lo.astype(F32), 2 * C_HEADS, axis=1))
    q_side = pieces + ((lane >= AUG_HALF) & (lane < 2 * AUG_HALF)).astype(F32)
    k_side = (lane < AUG_HALF).astype(F32) - pltpu.roll(pieces, AUG_HALF, axis=1)
    return q_side.astype(BF16), k_side.astype(BF16)


def _head_lanes(k_side, head):
    lane = lax.broadcasted_iota(jnp.int32, (1, LANES), 1)
    return jnp.where((lane % C_HEADS == head) & (lane < 2 * AUG_HALF), k_side, jnp.zeros_like(k_side))


def _fox_proj_body(x_ref, g_ref, wq_ref, wk_ref, wv_ref, wz_ref, wf_ref, gq_ref, gk_ref, bf_ref, *refs,
                   slot, first, with_aug):
    if not first:
        refs = refs[2:]
    q_ref, kf_ref, kb_ref, vf_ref, vb_ref, z_ref, lf_ref = refs[:7]
    tm = x_ref.shape[0]
    if with_aug:
        qa_ref, ka_ref, carry_sc = refs[7:]
        _reset_at_stream_start(carry_sc)
    if first:
        for s in range(kf_ref.shape[0]):
            if s != slot:
                kf_ref[s] = jnp.zeros(kf_ref.shape[1:], F32)
                vf_ref[s] = jnp.zeros(vf_ref.shape[1:], F32)
        kf_ref, vf_ref = kf_ref.at[slot], vf_ref.at[slot]
    h = _rms(x_ref[...], g_ref[...]).astype(BF16)
    q = _dot(h, wq_ref[...])
    k = _dot(h, wk_ref[...])
    v = _dot(h, wv_ref[...])
    q_scale = C_HEAD_DIM ** -0.5 * LOG2E
    kn = []
    for hd in range(C_HEADS):
        sl = slice(hd * C_HEAD_DIM, (hd + 1) * C_HEAD_DIM)
        q_ref[:, sl] = (_rms(q[:, sl], gq_ref[...]) * q_scale).astype(BF16)
        kn.append(_rms(k[:, sl], gk_ref[...]))
    kn = jnp.concatenate(kn, axis=1)
    kf_ref[...] = kn.reshape(tm, C_HEADS, C_HEAD_DIM)
    kb_ref[...] = kn.astype(BF16)
    vf_ref[...] = v.reshape(tm, C_HEADS, C_HEAD_DIM)
    vb_ref[...] = (v.T if with_aug else v).astype(BF16)
    z_ref[...] = _dot(h, wz_ref[...])
    lf = jax.nn.log_sigmoid(_dot(h, wf_ref[...]) + bf_ref[...])
    lf_ref[...] = lf
    if with_aug:
        qa_ref[...], ka_ref[...] = _forget_lanes(_forget_cumsum(lf, carry_sc))


def fox_proj(x, g, wq, wk, wv, wz, wf, gq, gk, bf, tm, n_slots, slot, prev_k, prev_v, with_aug):
    b, l, d = x.shape
    tok = lambda w: pl.BlockSpec((None, tm, w), lambda i, j: (i, j, 0))
    act = lambda w, dt: jax.ShapeDtypeStruct((b, l, w), dt)
    ins = [g, wq, wk, wv, wz, wf, gq, gk, bf]
    first = prev_k is None
    stacked = jax.ShapeDtypeStruct((n_slots, b, l, C_HEADS, C_HEAD_DIM), F32)
    if first:
        st_spec = pl.BlockSpec((n_slots, None, tm, C_HEADS, C_HEAD_DIM), lambda i, j: (0, i, j, 0, 0))
        extra, extra_specs, aliases = [], [], {}
    else:
        st_spec = pl.BlockSpec((None, None, tm, C_HEADS, C_HEAD_DIM), lambda i, j: (slot, i, j, 0, 0))
        extra, extra_specs = [prev_k, prev_v], [pl.BlockSpec(memory_space=pl.ANY)] * 2
        aliases = {1 + len(ins): 1, 2 + len(ins): 3}
    w = C_HEADS * C_HEAD_DIM
    aug_specs, aug_shapes, scratch = [], [], []
    v_spec, v_shape = tok(w), act(w, BF16)
    if with_aug:
        assert tm == VT_KEYS
        aug_specs, aug_shapes = [tok(LANES)] * 2, [act(LANES, BF16)] * 2
        scratch = [pltpu.VMEM((1, LANES), F32)]
        v_spec = pl.BlockSpec((None, None, w, tm), lambda i, j: (i, j, 0, 0))
        v_shape = jax.ShapeDtypeStruct((b, l // tm, w, tm), BF16)
    return pl.pallas_call(
        functools.partial(_fox_proj_body, slot=slot, first=first, with_aug=with_aug),
        grid=(b, l // tm),
        in_specs=[tok(d)] + [_const_spec(a.shape) for a in ins] + extra_specs,
        out_specs=[tok(w), st_spec, tok(w), st_spec, v_spec, tok(w), tok(LANES)] + aug_specs,
        out_shape=[act(w, BF16), stacked, act(w, BF16), stacked, v_shape, act(w, F32), act(LANES, F32)]
        + aug_shapes,
        input_output_aliases=aliases,
        scratch_shapes=scratch,
        compiler_params=_params(("parallel", "arbitrary")),
        name="fox_proj",
    )(x, *ins, *extra)


def _fox_aug_body(lf_ref, qa_ref, ka_ref, carry_sc):
    _reset_at_stream_start(carry_sc)
    qa_ref[...], ka_ref[...] = _forget_lanes(_forget_cumsum(lf_ref[...], carry_sc))


def fox_aug(lf, ts):
    b, s, _ = lf.shape
    return pl.pallas_call(
        _fox_aug_body,
        grid=(b, s // ts),
        in_specs=[pl.BlockSpec((None, ts, LANES), lambda i, j: (i, j, 0))],
        out_specs=[pl.BlockSpec((None, ts, LANES), lambda i, j: (i, j, 0))] * 2,
        out_shape=[jax.ShapeDtypeStruct((b, s, LANES), BF16)] * 2,
        scratch_shapes=[pltpu.VMEM((1, LANES), F32)],
        compiler_params=_params(("parallel", "arbitrary")),
        name="fox_aug",
    )(lf)


ATTN_ROWS = 1024
ATTN_CHAINS = 2


def _softmax_step(qc, carry, kc, vts, diagonal):
    m, l, acc = carry
    st = _dot_nt(kc, qc)
    if diagonal:
        row = lax.broadcasted_iota(jnp.int32, st.shape, 0)
        col = lax.broadcasted_iota(jnp.int32, st.shape, 1)
        st = jnp.where(row <= col, st, -jnp.inf)
    m_new = jnp.maximum(m, jnp.max(st, axis=0, keepdims=True))
    alpha = jnp.exp2(m - m_new)
    p = jnp.exp2(st - m_new)
    l = alpha * l + jnp.sum(p, axis=0, keepdims=True)
    pb = p.astype(BF16)
    pv = _dot(vts[0], pb[:VT_KEYS, :])
    for u in range(1, len(vts)):
        pv = pv + _dot(vts[u], pb[u * VT_KEYS:(u + 1) * VT_KEYS, :])
    return m_new, l, alpha * acc + pv


def _fox_attn_body(q_ref, qa_ref, k_ref, ka_ref, vt_ref, z_ref, o_ref, *, rows, chains, part):
    qi = pl.program_id(2)
    tq = rows * chains
    tiles = lambda n: n // VT_KEYS

    def queries(r0, n):
        return jnp.concatenate([q_ref[r0:r0 + n, :], qa_ref[r0:r0 + n, :]], axis=1)

    def keys(r0, n):
        return jnp.concatenate([k_ref[pl.ds(r0, n), :], _head_lanes(ka_ref[pl.ds(r0, n), :], pl.program_id(1))],
                               axis=1)

    qcs = [queries(c * rows, rows) for c in range(chains)]

    def below_diagonal(j, carries):
        kc = keys(pl.multiple_of(j * tq, tq), tq)
        vts = [vt_ref[j * tiles(tq) + u] for u in range(tiles(tq))]
        return tuple(_softmax_step(qcs[c], carries[c], kc, vts, False) for c in range(chains))

    init = tuple((jnp.full((1, rows), -jnp.inf, F32), jnp.zeros((1, rows), F32), jnp.zeros((C_HEAD_DIM, rows), F32))
                 for _ in range(chains))
    carries = lax.fori_loop(0, qi, below_diagonal, init)

    base = pl.multiple_of(qi * tq, tq)
    kc = keys(base, tq)
    vts = [vt_ref[qi * tiles(tq) + u] for u in range(tiles(tq))]
    for i in range(tq // part):
        c, o = divmod(i * part, rows)
        carry = tuple(a[:, o:o + part] for a in carries[c])
        qc = queries(i * part, part)
        if i > 0:
            carry = _softmax_step(qc, carry, kc[:i * part, :], vts[:tiles(i * part)], False)
        _, l, acc = _softmax_step(qc, carry, kc[i * part:(i + 1) * part, :],
                                  vts[tiles(i * part):tiles((i + 1) * part)], True)
        sl = slice(i * part, (i + 1) * part)
        o_ref[sl, :] = (_silu(z_ref[sl, :]) * (acc / l).T).astype(o_ref.dtype)


def fox_attn(q, qa, k, ka, vt, z):
    b, l, w = q.shape
    chains = ATTN_CHAINS if l % (ATTN_ROWS * ATTN_CHAINS) == 0 else 1
    rows = ATTN_ROWS if l % (ATTN_ROWS * chains) == 0 else l // chains
    tq = rows * chains
    part = rows
    qspec = pl.BlockSpec((None, tq, C_HEAD_DIM), lambda i, h, j: (i, j, h))
    qaspec = pl.BlockSpec((None, tq, LANES), lambda i, h, j: (i, j, 0))
    kspec = pl.BlockSpec((None, l, C_HEAD_DIM), lambda i, h, j: (i, 0, h))
    kaspec = pl.BlockSpec((None, l, LANES), lambda i, h, j: (i, 0, 0))
    vspec = pl.BlockSpec((None, l // VT_KEYS, C_HEAD_DIM, VT_KEYS), lambda i, h, j: (i, 0, h, 0))
    return pl.pallas_call(
        functools.partial(_fox_attn_body, rows=rows, chains=chains, part=part),
        grid=(b, C_HEADS, l // tq),
        in_specs=[qspec, qaspec, kspec, kaspec, vspec, qspec],
        out_specs=qspec,
        out_shape=jax.ShapeDtypeStruct((b, l, w), BF16),
        compiler_params=_params(("parallel", "parallel", "arbitrary")),
        name="fox_attn",
    )(q, qa, k, ka, vt, z)


def _fox_attn_sample_body(q_ref, qa_ref, kp_ref, kap_ref, vp_ref, kn_ref, kan_ref, vn_ref, z_ref, o_ref, *, p0):
    n = q_ref.shape[0]
    for hd in range(C_HEADS):
        sl = slice(hd * C_HEAD_DIM, (hd + 1) * C_HEAD_DIM)
        head_rows = pl.ds(hd, p0, stride=C_HEADS)
        qc = jnp.concatenate([q_ref[:, sl], qa_ref[...]], axis=1)
        s_past = _dot_nt(qc, jnp.concatenate([kp_ref[head_rows, :].astype(BF16), _head_lanes(kap_ref[...], hd)],
                                             axis=1))
        s_new = _dot_nt(qc, jnp.concatenate([kn_ref[:, sl], _head_lanes(kan_ref[...], hd)], axis=1))
        s_new = jnp.where(_tril(n), s_new, -jnp.inf)
        m = jnp.maximum(jnp.max(s_past, axis=-1, keepdims=True), jnp.max(s_new, axis=-1, keepdims=True))
        p_past = jnp.exp2(s_past - m)
        p_new = jnp.exp2(s_new - m)
        l = jnp.sum(p_past, axis=-1, keepdims=True) + jnp.sum(p_new, axis=-1, keepdims=True)
        acc = (_dot(p_past.astype(BF16), vp_ref[head_rows, :].astype(BF16))
               + _dot(p_new.astype(BF16), vn_ref[:, sl]))
        o_ref[:, sl] = (_silu(z_ref[:, sl]) * (acc / l)).astype(o_ref.dtype)


def fox_attn_sample(q, qa_all, cache_k, ka_all, cache_v, layer, k_new, v_new, z):
    b, n, w = q.shape
    p0 = cache_k.shape[2] // C_HEADS
    new = pl.BlockSpec((None, n, w), lambda i: (i, 0, 0))
    new_of_all = pl.BlockSpec((None, n, LANES), lambda i: (i, p0 // n, 0))
    past_aug = pl.BlockSpec((None, p0, LANES), lambda i: (i, 0, 0))
    past = pl.BlockSpec((None, None, p0 * C_HEADS, C_HEAD_DIM), lambda i: (layer, i, 0, 0))
    return pl.pallas_call(
        functools.partial(_fox_attn_sample_body, p0=p0),
        grid=(b,),
        in_specs=[new, new_of_all, past, past_aug, past, new, new_of_all, new, new],
        out_specs=new,
        out_shape=jax.ShapeDtypeStruct((b, n, w), BF16),
        compiler_params=_params(("parallel",)),
        name="fox_attn_sample",
    )(q, qa_all, cache_k, ka_all, cache_v, k_new, ka_all, v_new, z)


def _largest_tile(n, cap, mult):
    t = min(n, cap)
    while n % t or t % mult:
        t -= 1
    return t


def _pad_cols(a, width):
    return jnp.pad(a, ((0, 0), (0, width - a.shape[1])))


def _even_weights(w_in, w_out, ws, bs, gv, conv_w, conv_b, dt_bias, a_log, d_skip, g_ssd):
    cuts = (A_WIDTH, 2 * A_WIDTH, 3 * A_WIDTH, 3 * A_WIDTH + B_WIDTH, 3 * A_WIDTH + B_WIDTH + B_CONV_DIM)
    wu, wv, wza, wzb, wx, wdt = jnp.split(w_in, cuts, axis=1)
    e_mat = np.zeros((LANES, B_HEADS * B_HEAD_DIM), np.float32)
    for h in range(B_HEADS):
        e_mat[h, h * B_HEAD_DIM:(h + 1) * B_HEAD_DIM] = 1.0
    return dict(
        w_in=[p.astype(BF16) for p in (wu, wv, wza, wzb, wx, _pad_cols(wdt, LANES))], w_out=w_out.astype(BF16),
        ws=ws, bs=bs, gv=gv, cw=conv_w.T, cb=conv_b[None, :], dtb=_pad_cols(dt_bias[None, :], LANES),
        alog=_pad_cols(a_log[None, :], LANES), dskip_e=jnp.repeat(d_skip, B_HEAD_DIM)[None, :],
        gssd=g_ssd[None, :], e_mat=jnp.asarray(e_mat, BF16))


def _even_layer(x, g_pre, g_post, wts, conv0, s0, emit_v):
    b, l, d = x.shape
    t = b * l
    if l >= PROJ_ROWS:
        xin, tm, seg = x, _largest_tile(l, PROJ_ROWS, 16), _largest_tile(l, PROJ_ROWS, 16)
    else:
        xin, tm, seg = x.reshape(1, t, d), t, l
    outs = even_proj(xin, g_pre[None, :], wts["w_in"], wts["gv"], wts["cw"], wts["cb"], wts["dtb"],
                     conv0.reshape(xin.shape[0], tm // seg, B_CONV - 1, B_CONV_DIM), tm, seg, emit_v)
    gate, vb, *vrows, zs, xc, dt = (a.reshape(b, l, a.shape[2]) for a in outs[:-1])
    cfin = outs[-1].reshape(b, B_CONV - 1, B_CONV_DIM)
    blk = min(l, A_CHUNK)
    ycat, sfin = even_mixer(gate, vb, zs, xc, dt, wts["ws"][:, :blk, :blk], wts["bs"][:, :blk].T, wts["alog"],
                            wts["dskip_e"], wts["gssd"], wts["e_mat"],
                            s0.reshape(b, B_HEADS * B_HEAD_DIM, B_STATE), blk)
    xn = out_proj(ycat.reshape(t, ycat.shape[2]), wts["w_out"], x.reshape(t, d), g_post[None, :],
                  _largest_tile(t, 512, 8))
    return (xn.reshape(b, l, d), *vrows, sfin.reshape(b, B_HEADS, B_HEAD_DIM, B_STATE), cfin)


def _odd_weights(w_in, b_f, w_out, gq, gk):
    w = C_HEADS * C_HEAD_DIM
    wq, wk, wv, wz, wf = jnp.split(w_in, (w, 2 * w, 3 * w, 4 * w), axis=1)
    return dict(wq=wq.astype(BF16), wk=wk.astype(BF16), wv=wv.astype(BF16), wz=wz.astype(BF16),
                wf=_pad_cols(wf, LANES).astype(BF16), bf=_pad_cols(b_f[None, :], LANES),
                w_out=w_out.astype(BF16), gq=gq[None, :], gk=gk[None, :])


def _odd_layer(x, g_pre, g_post, wts, n_slots, slot, prev_k, prev_v, cache):
    b, l, d = x.shape
    t = b * l
    prompt = cache is None
    xin = x if prompt else x.reshape(1, t, d)
    q, kst, kb, vst, vb, z, lf, *aug = fox_proj(
        xin, g_pre[None, :], wts["wq"], wts["wk"], wts["wv"], wts["wz"], wts["wf"], wts["gq"], wts["gk"],
        wts["bf"], _largest_tile(xin.shape[1], PROJ_ROWS, 16), n_slots, slot, prev_k, prev_v, prompt)
    if prompt:
        yc = fox_attn(q, aug[0], kb, aug[1], vb, z)
    else:
        q, kb, vb, z, lf = (a.reshape(b, l, a.shape[2]) for a in (q, kb, vb, z, lf))
        cache_k, cache_v, cache_logf = cache
        p0 = cache_k.shape[2]
        past_lf = _pad_cols(cache_logf[slot].reshape(b * p0, C_HEADS), LANES).reshape(b, p0, LANES)
        qa, ka = fox_aug(jnp.concatenate([past_lf, lf], axis=1), _largest_tile(p0 + l, 1040, 16))
        cache_rows = lambda c: c.reshape(c.shape[0], b, p0 * C_HEADS, C_HEAD_DIM)
        yc = fox_attn_sample(q, qa, cache_rows(cache_k), ka, cache_rows(cache_v), slot, kb, vb, z)
    xn = out_proj(yc.reshape(t, d), wts["w_out"], x.reshape(t, d), g_post[None, :], _largest_tile(t, 512, 8))
    return xn.reshape(b, l, d), kst, vst, lf[:, :, :C_HEADS]


def kernel(x_prompt, x_sample, cache_fox_k, cache_fox_v, cache_fox_logf, state_ssd, state_conv, norm_pre, norm_post, w_in_even, w_out_even, gmlp_ws, gmlp_bs, gmlp_gv, ssd_conv_w, ssd_conv_b, ssd_dt_bias, ssd_a_log, ssd_d, ssd_norm_g, w_in_odd, fox_b_forget, w_out_odd, fox_gq, fox_gk):
    xp, xs = x_prompt, x_sample
    bp = xp.shape[0]
    depth = norm_pre.shape[0]
    n_odd = depth // 2
    outs = {n: [] for n in ("lp", "sp", "cp", "ls", "ss", "cs", "gs")}
    kp = vp = ks = vs = None
    cache = (cache_fox_k, cache_fox_v, cache_fox_logf)
    for i in range(depth):
        j = i // 2
        if i % 2 == 0:
            wts = _even_weights(w_in_even[j], w_out_even[j], gmlp_ws[j], gmlp_bs[j], gmlp_gv[j], ssd_conv_w[j],
                                ssd_conv_b[j], ssd_dt_bias[j], ssd_a_log[j], ssd_d[j], ssd_norm_g[j])
            xp, sp, cp = _even_layer(xp, norm_pre[i], norm_post[i], wts,
                                     jnp.zeros((bp, B_CONV - 1, B_CONV_DIM), F32),
                                     jnp.zeros((bp, B_HEADS, B_HEAD_DIM, B_STATE), F32), False)
            xs, gs, ss, cs = _even_layer(xs, norm_pre[i], norm_post[i], wts, state_conv[j], state_ssd[j], True)
            for n, a in (("sp", sp), ("cp", cp), ("ss", ss), ("cs", cs), ("gs", gs)):
                outs[n].append(a)
        else:
            wts = _odd_weights(w_in_odd[j], fox_b_forget[j], w_out_odd[j], fox_gq[j], fox_gk[j])
            xp, kp, vp, lp = _odd_layer(xp, norm_pre[i], norm_post[i], wts, n_odd, j, kp, vp, None)
            xs, ks, vs, ls = _odd_layer(xs, norm_pre[i], norm_post[i], wts, n_odd, j, ks, vs, cache)
            outs["lp"].append(lp)
            outs["ls"].append(ls)
    st = {n: jnp.stack(a) for n, a in outs.items()}
    ks, vs = (a.reshape((n_odd,) + xs.shape[:2] + (C_HEADS, C_HEAD_DIM)) for a in (ks, vs))
    return (xp, xs, kp, vp, st["lp"], st["sp"], st["cp"], ks, vs, st["ls"], st["ss"], st["cs"], st["gs"])
```

```python
import functools

import numpy as np
import jax
import jax.numpy as jnp
from jax import lax
from jax.experimental import pallas as pl
from jax.experimental.pallas import tpu as pltpu

F32 = jnp.float32
BF16 = jnp.bfloat16
EPS = 1e-6
LANES = 128
VMEM_LIMIT = 56 * 1024 * 1024

D_MODEL = 1024
A_GROUPS = 4
A_GROUP_DIM = 256
A_WIDTH = A_GROUPS * A_GROUP_DIM
A_CHUNK = 128
SUB_CHUNK = 64
B_HEADS = 16
B_HEAD_DIM = 64
B_GROUPS = 2
B_STATE = 128
B_CONV = 4
B_WIDTH = 1024
B_CONV_DIM = B_WIDTH + 2 * B_GROUPS * B_STATE
CTX_ROW = 8
C_HEADS = 8
C_HEAD_DIM = 128
SQRT_HALF = 0.7071067811865476
LOG2E = 1.4426950408889634
AUG_PIECES = 3
AUG_HALF = AUG_PIECES * C_HEADS
MIXER_BLOCKS_PER_STEP = 4
PROJ_ROWS = 256
VT_KEYS = PROJ_ROWS


def _params(sem):
    return pltpu.CompilerParams(dimension_semantics=sem, vmem_limit_bytes=VMEM_LIMIT)


def _const_spec(shape):
    nd = len(shape)
    return pl.BlockSpec(shape, lambda *_: (0,) * nd, pipeline_mode=pl.Buffered(1))


def _dot(a, b):
    return jnp.dot(a, b, preferred_element_type=F32)


def _dot_nt(a, b):
    return lax.dot_general(a, b, (((1,), (1,)), ((), ())), preferred_element_type=F32)


def _split3(x):
    hi = x.astype(BF16)
    r = x - hi.astype(F32)
    mid = r.astype(BF16)
    lo = (r - mid.astype(F32)).astype(BF16)
    return hi, mid, lo


def _dot3_lhs(x, w):
    hi, mid, lo = _split3(x)
    return _dot(hi, w) + _dot(mid, w) + _dot(lo, w)


def _dot3_rhs(w, x):
    hi, mid, lo = _split3(x)
    return _dot(w, hi) + _dot(w, mid) + _dot(w, lo)


def _rms(x, g):
    return x * lax.rsqrt(jnp.mean(x * x, axis=-1, keepdims=True) + EPS) * g


def _gelu(x):
    return 0.5 * x * (1.0 + lax.erf(x * SQRT_HALF))


def _silu(x):
    hx = 0.5 * x
    return hx + hx * jnp.tanh(hx)


def _tril(n):
    row = lax.broadcasted_iota(jnp.int32, (n, n), 0)
    col = lax.broadcasted_iota(jnp.int32, (n, n), 1)
    return col <= row


def _layer_input(pre_refs, xnew_ref):
    if len(pre_refs) == 1:
        return pre_refs[0][...]
    y_ref, w_ref, x_ref, g_ref = pre_refs
    x = x_ref[...] + _rms(_dot(y_ref[...], w_ref[...]), g_ref[...])
    xnew_ref[...] = x
    return x


def _pre_specs(pre, tok):
    if len(pre) == 1:
        return [tok(pre[0].shape[2])], [], []
    y, w, x, g = pre
    return ([tok(y.shape[2]), _const_spec(w.shape), tok(x.shape[2]), _const_spec(g.shape)],
            [tok(x.shape[2])], [jax.ShapeDtypeStruct(x.shape, F32)])


def _even_proj_body(*refs, tm, seg, emit_v, n_pre):
    pre_refs, refs = refs[:n_pre], refs[n_pre:]
    (g_ref, wu_ref, wv_ref, wza_ref, wzb_ref, wx_ref, wdt_ref, gv_ref, cw_ref, cb_ref, dtb_ref,
     conv0_ref), refs = refs[:12], refs[12:]
    xnew_ref, refs = (refs[0], refs[1:]) if n_pre > 1 else (None, refs)
    gate_ref, vb_ref, zs_ref, xc_ref, dt_ref, cfin_ref, prev_sc = refs[:2] + refs[-5:]
    j = pl.program_id(1)
    nseg = tm // seg
    tail = lambda s: slice(s * CTX_ROW + CTX_ROW - (B_CONV - 1), (s + 1) * CTX_ROW)

    @pl.when(j == 0)
    def _():
        prev_sc[...] = jnp.zeros_like(prev_sc)
        for s in range(nseg):
            prev_sc[tail(s), :] = conv0_ref[s]

    h = _rms(_layer_input(pre_refs, xnew_ref), g_ref[...]).astype(BF16)
    xbc = _dot(h, wx_ref[...])
    span = CTX_ROW + seg
    ext = jnp.concatenate([piece for s in range(nseg) for piece in
                           (prev_sc[s * CTX_ROW:(s + 1) * CTX_ROW, :], xbc[s * seg:(s + 1) * seg, :])], axis=0)
    acc = cb_ref[...] + xbc * cw_ref[B_CONV - 1:B_CONV, :]
    for back in range(1, B_CONV):
        rolled = pltpu.roll(ext, back, axis=0)
        shifted = jnp.concatenate([rolled[s * span + CTX_ROW:(s + 1) * span, :] for s in range(nseg)], axis=0)
        acc = acc + shifted * cw_ref[B_CONV - 1 - back:B_CONV - back, :]
    xc_ref[...] = _silu(acc)
    for s in range(nseg):
        prev_sc[s * CTX_ROW:(s + 1) * CTX_ROW, :] = xbc[(s + 1) * seg - CTX_ROW:(s + 1) * seg, :]
    gv = _gelu(_dot(h, wv_ref[...]))
    vn = jnp.concatenate(
        [_rms(gv[:, g * A_GROUP_DIM:(g + 1) * A_GROUP_DIM], gv_ref[g:g + 1, :]) for g in range(A_GROUPS)], axis=1)
    vb_ref[...] = vn.astype(BF16)
    if emit_v:
        refs[2][...] = vn
    gate_ref[...] = _silu(_dot(h, wza_ref[...])) * _gelu(_dot(h, wu_ref[...]))
    zs_ref[...] = _silu(_dot(h, wzb_ref[...]))
    dt_ref[...] = jax.nn.softplus(_dot(h, wdt_ref[...]) + dtb_ref[...])

    @pl.when(j == pl.num_programs(1) - 1)
    def _():
        for s in range(nseg):
            cfin_ref[s] = prev_sc[tail(s), :]


def even_proj(pre, g, ws, gv, cw, cb, dtb, conv0, tm, seg, emit_v):
    b, l, _ = pre[-2 if len(pre) > 1 else 0].shape
    nseg = tm // seg
    assert nseg == 1 or l == tm
    tok = lambda w: pl.BlockSpec((None, tm, w), lambda i, j: (i, j, 0))
    ctx = pl.BlockSpec((None, nseg, B_CONV - 1, B_CONV_DIM), lambda i, j: (i, 0, 0, 0))
    consts = [g] + list(ws) + [gv, cw, cb, dtb]
    act = lambda w, dt: jax.ShapeDtypeStruct((b, l, w), dt)
    v_spec, v_shape = ([tok(A_WIDTH)], [act(A_WIDTH, F32)]) if emit_v else ([], [])
    pre_in, pre_out, pre_shape = _pre_specs(pre, tok)
    return pl.pallas_call(
        functools.partial(_even_proj_body, tm=tm, seg=seg, emit_v=emit_v, n_pre=len(pre)),
        grid=(b, l // tm),
        in_specs=pre_in + [_const_spec(c.shape) for c in consts] + [ctx],
        out_specs=pre_out + [tok(A_WIDTH), tok(A_WIDTH)] + v_spec
        + [tok(B_WIDTH), tok(B_CONV_DIM), tok(LANES), ctx],
        out_shape=pre_shape + [act(A_WIDTH, F32), act(A_WIDTH, BF16)] + v_shape
        + [act(B_WIDTH, F32), act(B_CONV_DIM, F32), act(LANES, F32),
           jax.ShapeDtypeStruct((b, nseg, B_CONV - 1, B_CONV_DIM), F32)],
        scratch_shapes=[pltpu.VMEM((nseg * CTX_ROW, B_CONV_DIM), F32)],
        compiler_params=_params(("parallel", "arbitrary")),
        name="even_proj",
    )(*pre, *consts, conv0)


def _out_proj_body(y_ref, w_ref, x_ref, g_ref, o_ref):
    o = _dot(y_ref[...], w_ref[...])
    o_ref[...] = x_ref[...] + _rms(o, g_ref[...])


def out_proj(y, w, x, g, tm):
    t, d = x.shape
    kd = y.shape[1]
    return pl.pallas_call(
        _out_proj_body,
        grid=(t // tm,),
        in_specs=[pl.BlockSpec((tm, kd), lambda i: (i, 0)), _const_spec(w.shape),
                  pl.BlockSpec((tm, d), lambda i: (i, 0)), _const_spec((1, d))],
        out_specs=pl.BlockSpec((tm, d), lambda i: (i, 0)),
        out_shape=jax.ShapeDtypeStruct((t, d), F32),
        compiler_params=_params(("parallel",)),
        name="out_proj",
    )(y, w, x, g)


def _even_mixer_body(gate_ref, vb_ref, zs_ref, xc_ref, dt_ref, ws_ref, bs_ref, alog_ref, dskip_ref, gssd_ref,
                     e_ref, s0_ref, y_ref, sfin_ref, st_sc, *, blk, nsub):
    j = pl.program_id(1)

    @pl.when(j == 0)
    def _():
        st_sc[...] = s0_ref[...].T

    for sub in range(nsub):
        _mixer_block(slice(sub * blk, (sub + 1) * blk), blk, gate_ref, vb_ref, zs_ref, xc_ref, dt_ref, ws_ref,
                     bs_ref, alog_ref, dskip_ref, gssd_ref, e_ref, y_ref, st_sc)

    @pl.when(j == pl.num_programs(1) - 1)
    def _():
        sfin_ref[...] = st_sc[...].T


def _mixer_block(rs, blk, gate_ref, vb_ref, zs_ref, xc_ref, dt_ref, ws_ref, bs_ref, alog_ref, dskip_ref, gssd_ref,
                 e_ref, y_ref, st_sc):
    row = lax.broadcasted_iota(jnp.int32, (blk, blk), 0)
    col = lax.broadcasted_iota(jnp.int32, (blk, blk), 1)
    causal = col <= row

    chunk_causal = (col // SUB_CHUNK) <= (row // SUB_CHUNK)
    s_parts = []
    for g in range(A_GROUPS):
        wg = jnp.where(chunk_causal, ws_ref[g], 0.0).astype(BF16)
        sg = _dot(wg, vb_ref[rs, g * A_GROUP_DIM:(g + 1) * A_GROUP_DIM])
        s_parts.append(sg + bs_ref[:, g:g + 1])
    ya = gate_ref[rs, :] * jnp.concatenate(s_parts, axis=1)

    xs = xc_ref[rs, :B_WIDTH]
    bm = xc_ref[rs, B_WIDTH:B_WIDTH + B_GROUPS * B_STATE]
    cm = xc_ref[rs, B_WIDTH + B_GROUPS * B_STATE:]
    lane = lax.broadcasted_iota(jnp.int32, (1, LANES), 1)
    dt = dt_ref[rs, :]
    a = jnp.where(lane < B_HEADS, -jnp.exp(alog_ref[...]), 0.0)
    acs = _dot3_rhs(causal.astype(BF16), dt * a)
    acs_t = acs.T
    e = e_ref[...]
    dt_e = _dot3_lhs(dt, e)
    acs_e = _dot3_lhs(acs, e)
    expacs_e = jnp.exp(acs_e)
    toend_e = jnp.exp(acs_e[blk - 1:blk, :] - acs_e)
    blkdec_e = expacs_e[blk - 1:blk, :]
    dtx = dt_e * xs
    wds = (toend_e * dtx).astype(BF16)
    dtx_b = dtx.astype(BF16)
    lane_hp = lax.broadcasted_iota(jnp.int32, (blk, LANES), 1)
    gw = (B_HEADS // B_GROUPS) * B_HEAD_DIM
    ys_parts, yi_parts = [], []
    for g in range(B_GROUPS):
        bm_g = bm[:, g * B_STATE:(g + 1) * B_STATE]
        cm_g = cm[:, g * B_STATE:(g + 1) * B_STATE].astype(BF16)
        cb = _dot_nt(cm_g, bm_g.astype(BF16))
        st_g = st_sc[:, g * gw:(g + 1) * gw]
        ds = _dot(bm_g.T.astype(BF16), wds[:, g * gw:(g + 1) * gw])
        st_sc[:, g * gw:(g + 1) * gw] = blkdec_e[:, g * gw:(g + 1) * gw] * st_g + ds
        ys_parts.append(_dot(cm_g, st_g.astype(BF16)))
        for hp in range(gw // LANES):
            c0 = g * gw + hp * LANES
            rhs = dtx_b[:, c0:c0 + LANES]
            pair = None
            for half in range(2):
                h = c0 // B_HEAD_DIM + half
                seg = acs[:, h:h + 1] - acs_t[h:h + 1, :]
                decay = jnp.exp(jnp.where(causal, seg, -jnp.inf))
                m_h = (cb * decay).astype(BF16)
                keep = (lane_hp < B_HEAD_DIM) if half == 0 else (lane_hp >= B_HEAD_DIM)
                part = _dot(m_h, jnp.where(keep, rhs, jnp.zeros_like(rhs)))
                pair = part if pair is None else pair + part
            yi_parts.append(pair)
    y = (jnp.concatenate(yi_parts, axis=1) + jnp.concatenate(ys_parts, axis=1) * expacs_e
         + dskip_ref[...] * xs)
    y = y * zs_ref[rs, :]
    half_w = B_WIDTH // B_GROUPS
    yb = jnp.concatenate(
        [_rms(y[:, g * half_w:(g + 1) * half_w], gssd_ref[:, g * half_w:(g + 1) * half_w])
         for g in range(B_GROUPS)], axis=1)
    y_ref[rs, :] = jnp.concatenate([ya, yb], axis=1).astype(y_ref.dtype)


def even_mixer(gate, vb, zs, xc, dt, ws, bs_t, alog, dskip_e, gssd, e_mat, s0, blk):
    b, l, _ = gate.shape
    nsub = MIXER_BLOCKS_PER_STEP if l % (MIXER_BLOCKS_PER_STEP * blk) == 0 else 1
    tok = lambda w: pl.BlockSpec((None, nsub * blk, w), lambda i, j: (i, j, 0))
    hp = B_HEADS * B_HEAD_DIM
    state = pl.BlockSpec((None, hp, B_STATE), lambda i, j: (i, 0, 0))
    consts = [ws, bs_t, alog, dskip_e, gssd, e_mat]
    return pl.pallas_call(
        functools.partial(_even_mixer_body, blk=blk, nsub=nsub),
        grid=(b, l // (nsub * blk)),
        in_specs=[tok(A_WIDTH), tok(A_WIDTH), tok(B_WIDTH), tok(B_CONV_DIM), tok(LANES)]
        + [_const_spec(c.shape) for c in consts] + [state],
        out_specs=[tok(A_WIDTH + B_WIDTH), state],
        out_shape=[jax.ShapeDtypeStruct((b, l, A_WIDTH + B_WIDTH), BF16), jax.ShapeDtypeStruct((b, hp, B_STATE), F32)],
        scratch_shapes=[pltpu.VMEM((B_STATE, hp), F32)],
        compiler_params=_params(("parallel", "arbitrary")),
        name="even_mixer",
    )(gate, vb, zs, xc, dt, *consts, s0)


def _reset_at_stream_start(carry_sc):
    @pl.when(pl.program_id(1) == 0)
    def _():
        carry_sc[...] = jnp.zeros_like(carry_sc)


def _forget_cumsum(lf, carry_sc):
    n = lf.shape[0]
    lane = lax.broadcasted_iota(jnp.int32, (1, LANES), 1)
    lf = jnp.where(lane < C_HEADS, lf, 0.0)
    chunk = _largest_tile(n, 256, 8)
    tri = _tril(chunk).astype(BF16)
    carry, parts = carry_sc[...], []
    for c in range(n // chunk):
        part = _dot3_rhs(tri, lf[c * chunk:(c + 1) * chunk, :]) + carry
        carry = part[chunk - 1:chunk, :]
        parts.append(part)
    carry_sc[...] = carry
    return jnp.concatenate(parts, axis=0)


def _forget_lanes(fc):
    lane = lax.broadcasted_iota(jnp.int32, (1, LANES), 1)
    hi, mid, lo = _split3(fc * LOG2E)
    pieces = (hi.astype(F32) + pltpu.roll(mid.astype(F32), C_HEADS, axis=1)
              + pltpu.roll(lo.astype(F32), 2 * C_HEADS, axis=1))
    q_side = pieces + ((lane >= AUG_HALF) & (lane < 2 * AUG_HALF)).astype(F32)
    k_side = (lane < AUG_HALF).astype(F32) - pltpu.roll(pieces, AUG_HALF, axis=1)
    return q_side.astype(BF16), k_side.astype(BF16)


def _head_lanes(k_side, head):
    lane = lax.broadcasted_iota(jnp.int32, (1, LANES), 1)
    return jnp.where((lane % C_HEADS == head) & (lane < 2 * AUG_HALF), k_side, jnp.zeros_like(k_side))


def _fox_proj_body(*refs, slot, first, with_aug, n_pre):
    pre_refs, refs = refs[:n_pre], refs[n_pre:]
    (g_ref, wq_ref, wk_ref, wv_ref, wz_ref, wf_ref, gq_ref, gk_ref, bf_ref), refs = refs[:9], refs[9:]
    if not first:
        refs = refs[2:]
    (q_ref, kf_ref, kb_ref, vf_ref, vb_ref, z_ref, lf_ref), refs = refs[:7], refs[7:]
    if with_aug:
        (qa_ref, ka_ref), refs = refs[:2], refs[2:]
    xnew_ref, refs = (refs[0], refs[1:]) if n_pre > 1 else (None, refs)
    tm = q_ref.shape[0]
    if with_aug:
        carry_sc = refs[0]
        _reset_at_stream_start(carry_sc)
    if first:
        for s in range(kf_ref.shape[0]):
            if s != slot:
                kf_ref[s] = jnp.zeros(kf_ref.shape[1:], F32)
                vf_ref[s] = jnp.zeros(vf_ref.shape[1:], F32)
        kf_ref, vf_ref = kf_ref.at[slot], vf_ref.at[slot]
    h = _rms(_layer_input(pre_refs, xnew_ref), g_ref[...]).astype(BF16)
    q = _dot(h, wq_ref[...])
    k = _dot(h, wk_ref[...])
    v = _dot(h, wv_ref[...])
    q_scale = C_HEAD_DIM ** -0.5 * LOG2E
    kn = []
    for hd in range(C_HEADS):
        sl = slice(hd * C_HEAD_DIM, (hd + 1) * C_HEAD_DIM)
        q_ref[:, sl] = (_rms(q[:, sl], gq_ref[...]) * q_scale).astype(BF16)
        kn.append(_rms(k[:, sl], gk_ref[...]))
    kn = jnp.concatenate(kn, axis=1)
    kf_ref[...] = kn.reshape(tm, C_HEADS, C_HEAD_DIM)
    kb_ref[...] = kn.astype(BF16)
    vf_ref[...] = v.reshape(tm, C_HEADS, C_HEAD_DIM)
    vb_ref[...] = (v.T if with_aug else v).astype(BF16)
    z_ref[...] = _dot(h, wz_ref[...])
    lf = jax.nn.log_sigmoid(_dot(h, wf_ref[...]) + bf_ref[...])
    lf_ref[...] = lf
    if with_aug:
        qa_ref[...], ka_ref[...] = _forget_lanes(_forget_cumsum(lf, carry_sc))


def fox_proj(pre, g, wq, wk, wv, wz, wf, gq, gk, bf, tm, n_slots, slot, prev_k, prev_v, with_aug):
    b, l, _ = pre[-2 if len(pre) > 1 else 0].shape
    tok = lambda w: pl.BlockSpec((None, tm, w), lambda i, j: (i, j, 0))
    act = lambda w, dt: jax.ShapeDtypeStruct((b, l, w), dt)
    ins = [g, wq, wk, wv, wz, wf, gq, gk, bf]
    first = prev_k is None
    stacked = jax.ShapeDtypeStruct((n_slots, b, l, C_HEADS, C_HEAD_DIM), F32)
    if first:
        st_spec = pl.BlockSpec((n_slots, None, tm, C_HEADS, C_HEAD_DIM), lambda i, j: (0, i, j, 0, 0))
        extra, extra_specs, aliases = [], [], {}
    else:
        st_spec = pl.BlockSpec((None, None, tm, C_HEADS, C_HEAD_DIM), lambda i, j: (slot, i, j, 0, 0))
        extra, extra_specs = [prev_k, prev_v], [pl.BlockSpec(memory_space=pl.ANY)] * 2
        aliases = {len(pre) + len(ins): 1, len(pre) + len(ins) + 1: 3}
    w = C_HEADS * C_HEAD_DIM
    aug_specs, aug_shapes, scratch = [], [], []
    v_spec, v_shape = tok(w), act(w, BF16)
    if with_aug:
        assert tm == VT_KEYS
        aug_specs, aug_shapes = [tok(LANES)] * 2, [act(LANES, BF16)] * 2
        scratch = [pltpu.VMEM((1, LANES), F32)]
        v_spec = pl.BlockSpec((None, None, w, tm), lambda i, j: (i, j, 0, 0))
        v_shape = jax.ShapeDtypeStruct((b, l // tm, w, tm), BF16)
    pre_in, pre_out, pre_shape = _pre_specs(pre, tok)
    return pl.pallas_call(
        functools.partial(_fox_proj_body, slot=slot, first=first, with_aug=with_aug, n_pre=len(pre)),
        grid=(b, l // tm),
        in_specs=pre_in + [_const_spec(a.shape) for a in ins] + extra_specs,
        out_specs=[tok(w), st_spec, tok(w), st_spec, v_spec, tok(w), tok(LANES)] + aug_specs + pre_out,
        out_shape=[act(w, BF16), stacked, act(w, BF16), stacked, v_shape, act(w, F32), act(LANES, F32)]
        + aug_shapes + pre_shape,
        input_output_aliases=aliases,
        scratch_shapes=scratch,
        compiler_params=_params(("parallel", "arbitrary")),
        name="fox_proj",
    )(*pre, *ins, *extra)


def _fox_aug_body(lf_ref, qa_ref, ka_ref, carry_sc):
    _reset_at_stream_start(carry_sc)
    qa_ref[...], ka_ref[...] = _forget_lanes(_forget_cumsum(lf_ref[...], carry_sc))


def fox_aug(lf, ts):
    b, s, _ = lf.shape
    return pl.pallas_call(
        _fox_aug_body,
        grid=(b, s // ts),
        in_specs=[pl.BlockSpec((None, ts, LANES), lambda i, j: (i, j, 0))],
        out_specs=[pl.BlockSpec((None, ts, LANES), lambda i, j: (i, j, 0))] * 2,
        out_shape=[jax.ShapeDtypeStruct((b, s, LANES), BF16)] * 2,
        scratch_shapes=[pltpu.VMEM((1, LANES), F32)],
        compiler_params=_params(("parallel", "arbitrary")),
        name="fox_aug",
    )(lf)


ATTN_ROWS = 1024
ATTN_CHAINS = 2


def _softmax_step(qc, carry, kc, vts, diagonal):
    m, l, acc = carry
    st = _dot_nt(kc, qc)
    if diagonal:
        row = lax.broadcasted_iota(jnp.int32, st.shape, 0)
        col = lax.broadcasted_iota(jnp.int32, st.shape, 1)
        st = jnp.where(row <= col, st, -jnp.inf)
    m_new = jnp.maximum(m, jnp.max(st, axis=0, keepdims=True))
    alpha = jnp.exp2(m - m_new)
    p = jnp.exp2(st - m_new)
    l = alpha * l + jnp.sum(p, axis=0, keepdims=True)
    pb = p.astype(BF16)
    pv = _dot(vts[0], pb[:VT_KEYS, :])
    for u in range(1, len(vts)):
        pv = pv + _dot(vts[u], pb[u * VT_KEYS:(u + 1) * VT_KEYS, :])
    return m_new, l, alpha * acc + pv


def _fox_attn_body(q_ref, qa_ref, k_ref, ka_ref, vt_ref, z_ref, o_ref, *, rows, chains, part):
    qi = pl.program_id(2)
    tq = rows * chains
    tiles = lambda n: n // VT_KEYS

    def queries(r0, n):
        return jnp.concatenate([q_ref[r0:r0 + n, :], qa_ref[r0:r0 + n, :]], axis=1)

    def keys(r0, n):
        return jnp.concatenate([k_ref[pl.ds(r0, n), :], _head_lanes(ka_ref[pl.ds(r0, n), :], pl.program_id(1))],
                               axis=1)

    qcs = [queries(c * rows, rows) for c in range(chains)]

    def below_diagonal(j, carries):
        kc = keys(pl.multiple_of(j * tq, tq), tq)
        vts = [vt_ref[j * tiles(tq) + u] for u in range(tiles(tq))]
        return tuple(_softmax_step(qcs[c], carries[c], kc, vts, False) for c in range(chains))

    init = tuple((jnp.full((1, rows), -jnp.inf, F32), jnp.zeros((1, rows), F32), jnp.zeros((C_HEAD_DIM, rows), F32))
                 for _ in range(chains))
    carries = lax.fori_loop(0, qi, below_diagonal, init)

    base = pl.multiple_of(qi * tq, tq)
    kc = keys(base, tq)
    vts = [vt_ref[qi * tiles(tq) + u] for u in range(tiles(tq))]
    for i in range(tq // part):
        c, o = divmod(i * part, rows)
        carry = tuple(a[:, o:o + part] for a in carries[c])
        qc = queries(i * part, part)
        if i > 0:
            carry = _softmax_step(qc, carry, kc[:i * part, :], vts[:tiles(i * part)], False)
        _, l, acc = _softmax_step(qc, carry, kc[i * part:(i + 1) * part, :],
                                  vts[tiles(i * part):tiles((i + 1) * part)], True)
        sl = slice(i * part, (i + 1) * part)
        o_ref[sl, :] = (_silu(z_ref[sl, :]) * (acc / l).T).astype(o_ref.dtype)


def fox_attn(q, qa, k, ka, vt, z):
    b, l, w = q.shape
    chains = ATTN_CHAINS if l % (ATTN_ROWS * ATTN_CHAINS) == 0 else 1
    rows = ATTN_ROWS if l % (ATTN_ROWS * chains) == 0 else l // chains
    tq = rows * chains
    part = rows
    qspec = pl.BlockSpec((None, tq, C_HEAD_DIM), lambda i, h, j: (i, j, h))
    qaspec = pl.BlockSpec((None, tq, LANES), lambda i, h, j: (i, j, 0))
    kspec = pl.BlockSpec((None, l, C_HEAD_DIM), lambda i, h, j: (i, 0, h))
    kaspec = pl.BlockSpec((None, l, LANES), lambda i, h, j: (i, 0, 0))
    vspec = pl.BlockSpec((None, l // VT_KEYS, C_HEAD_DIM, VT_KEYS), lambda i, h, j: (i, 0, h, 0))
    return pl.pallas_call(
        functools.partial(_fox_attn_body, rows=rows, chains=chains, part=part),
        grid=(b, C_HEADS, l // tq),
        in_specs=[qspec, qaspec, kspec, kaspec, vspec, qspec],
        out_specs=qspec,
        out_shape=jax.ShapeDtypeStruct((b, l, w), BF16),
        compiler_params=_params(("parallel", "parallel", "arbitrary")),
        name="fox_attn",
    )(q, qa, k, ka, vt, z)


def _fox_attn_sample_body(q_ref, qa_ref, kp_ref, kap_ref, vp_ref, kn_ref, kan_ref, vn_ref, z_ref, o_ref, *, p0):
    n = q_ref.shape[0]
    for hd in range(C_HEADS):
        sl = slice(hd * C_HEAD_DIM, (hd + 1) * C_HEAD_DIM)
        head_rows = pl.ds(hd, p0, stride=C_HEADS)
        qc = jnp.concatenate([q_ref[:, sl], qa_ref[...]], axis=1)
        s_past = _dot_nt(qc, jnp.concatenate([kp_ref[head_rows, :].astype(BF16), _head_lanes(kap_ref[...], hd)],
                                             axis=1))
        s_new = _dot_nt(qc, jnp.concatenate([kn_ref[:, sl], _head_lanes(kan_ref[...], hd)], axis=1))
        s_new = jnp.where(_tril(n), s_new, -jnp.inf)
        m = jnp.maximum(jnp.max(s_past, axis=-1, keepdims=True), jnp.max(s_new, axis=-1, keepdims=True))
        p_past = jnp.exp2(s_past - m)
        p_new = jnp.exp2(s_new - m)
        l = jnp.sum(p_past, axis=-1, keepdims=True) + jnp.sum(p_new, axis=-1, keepdims=True)
        acc = (_dot(p_past.astype(BF16), vp_ref[head_rows, :].astype(BF16))
               + _dot(p_new.astype(BF16), vn_ref[:, sl]))
        o_ref[:, sl] = (_silu(z_ref[:, sl]) * (acc / l)).astype(o_ref.dtype)


def fox_attn_sample(q, qa_all, cache_k, ka_all, cache_v, layer, k_new, v_new, z):
    b, n, w = q.shape
    p0 = cache_k.shape[2] // C_HEADS
    new = pl.BlockSpec((None, n, w), lambda i: (i, 0, 0))
    new_of_all = pl.BlockSpec((None, n, LANES), lambda i: (i, p0 // n, 0))
    past_aug = pl.BlockSpec((None, p0, LANES), lambda i: (i, 0, 0))
    past = pl.BlockSpec((None, None, p0 * C_HEADS, C_HEAD_DIM), lambda i: (layer, i, 0, 0))
    return pl.pallas_call(
        functools.partial(_fox_attn_sample_body, p0=p0),
        grid=(b,),
        in_specs=[new, new_of_all, past, past_aug, past, new, new_of_all, new, new],
        out_specs=new,
        out_shape=jax.ShapeDtypeStruct((b, n, w), BF16),
        compiler_params=_params(("parallel",)),
        name="fox_attn_sample",
    )(q, qa_all, cache_k, ka_all, cache_v, k_new, ka_all, v_new, z)


def _largest_tile(n, cap, mult):
    t = min(n, cap)
    while n % t or t % mult:
        t -= 1
    return t


def _pad_cols(a, width):
    return jnp.pad(a, ((0, 0), (0, width - a.shape[1])))


def _even_weights(w_in, w_out, ws, bs, gv, conv_w, conv_b, dt_bias, a_log, d_skip, g_ssd):
    cuts = (A_WIDTH, 2 * A_WIDTH, 3 * A_WIDTH, 3 * A_WIDTH + B_WIDTH, 3 * A_WIDTH + B_WIDTH + B_CONV_DIM)
    wu, wv, wza, wzb, wx, wdt = jnp.split(w_in, cuts, axis=1)
    e_mat = np.zeros((LANES, B_HEADS * B_HEAD_DIM), np.float32)
    for h in range(B_HEADS):
        e_mat[h, h * B_HEAD_DIM:(h + 1) * B_HEAD_DIM] = 1.0
    return dict(
        w_in=[p.astype(BF16) for p in (wu, wv, wza, wzb, wx, _pad_cols(wdt, LANES))], w_out=w_out.astype(BF16),
        ws=ws, bs=bs, gv=gv, cw=conv_w.T, cb=conv_b[None, :], dtb=_pad_cols(dt_bias[None, :], LANES),
        alog=_pad_cols(a_log[None, :], LANES), dskip_e=jnp.repeat(d_skip, B_HEAD_DIM)[None, :],
        gssd=g_ssd[None, :], e_mat=jnp.asarray(e_mat, BF16))


def _pre_rows(x, pend, groups, rows):
    rs = lambda a: a.reshape(groups, rows, a.shape[-1])
    return (rs(x),) if pend is None else (rs(pend[0]), pend[1], rs(x), pend[2])


def _even_layer(x, pend, g_pre, g_post, wts, conv0, s0, emit_v):
    b, l, d = x.shape
    t = b * l
    if l >= PROJ_ROWS:
        groups, rows, tm, seg = b, l, _largest_tile(l, PROJ_ROWS, 16), _largest_tile(l, PROJ_ROWS, 16)
    else:
        groups, rows, tm, seg = 1, t, t, l
    outs = even_proj(_pre_rows(x, pend, groups, rows), g_pre[None, :], wts["w_in"], wts["gv"], wts["cw"],
                     wts["cb"], wts["dtb"], conv0.reshape(groups, tm // seg, B_CONV - 1, B_CONV_DIM),
                     tm, seg, emit_v)
    if pend is not None:
        x, outs = outs[0].reshape(b, l, d), outs[1:]
    gate, vb, *vrows, zs, xc, dt = (a.reshape(b, l, a.shape[2]) for a in outs[:-1])
    cfin = outs[-1].reshape(b, B_CONV - 1, B_CONV_DIM)
    blk = min(l, A_CHUNK)
    ycat, sfin = even_mixer(gate, vb, zs, xc, dt, wts["ws"][:, :blk, :blk], wts["bs"][:, :blk].T, wts["alog"],
                            wts["dskip_e"], wts["gssd"], wts["e_mat"],
                            s0.reshape(b, B_HEADS * B_HEAD_DIM, B_STATE), blk)
    return (x, (ycat, wts["w_out"], g_post[None, :]), *vrows,
            sfin.reshape(b, B_HEADS, B_HEAD_DIM, B_STATE), cfin)


def _odd_weights(w_in, b_f, w_out, gq, gk):
    w = C_HEADS * C_HEAD_DIM
    wq, wk, wv, wz, wf = jnp.split(w_in, (w, 2 * w, 3 * w, 4 * w), axis=1)
    return dict(wq=wq.astype(BF16), wk=wk.astype(BF16), wv=wv.astype(BF16), wz=wz.astype(BF16),
                wf=_pad_cols(wf, LANES).astype(BF16), bf=_pad_cols(b_f[None, :], LANES),
                w_out=w_out.astype(BF16), gq=gq[None, :], gk=gk[None, :])


def _odd_layer(x, pend, g_pre, g_post, wts, n_slots, slot, prev_k, prev_v, cache):
    b, l, d = x.shape
    t = b * l
    prompt = cache is None
    groups, rows = (b, l) if prompt else (1, t)
    q, kst, kb, vst, vb, z, lf, *aug = fox_proj(
        _pre_rows(x, pend, groups, rows), g_pre[None, :], wts["wq"], wts["wk"], wts["wv"], wts["wz"], wts["wf"],
        wts["gq"], wts["gk"], wts["bf"], _largest_tile(rows, PROJ_ROWS, 16), n_slots, slot, prev_k, prev_v, prompt)
    if pend is not None:
        x, aug = aug[-1].reshape(b, l, d), aug[:-1]
    if prompt:
        yc = fox_attn(q, aug[0], kb, aug[1], vb, z)
    else:
        q, kb, vb, z, lf = (a.reshape(b, l, a.shape[2]) for a in (q, kb, vb, z, lf))
        cache_k, cache_v, cache_logf = cache
        p0 = cache_k.shape[2]
        past_lf = _pad_cols(cache_logf[slot].reshape(b * p0, C_HEADS), LANES).reshape(b, p0, LANES)
        qa, ka = fox_aug(jnp.concatenate([past_lf, lf], axis=1), _largest_tile(p0 + l, 1040, 16))
        cache_rows = lambda c: c.reshape(c.shape[0], b, p0 * C_HEADS, C_HEAD_DIM)
        yc = fox_attn_sample(q, qa, cache_rows(cache_k), ka, cache_rows(cache_v), slot, kb, vb, z)
    return x, (yc, wts["w_out"], g_post[None, :]), kst, vst, lf[:, :, :C_HEADS]


def _finish(x, pend):
    b, l, d = x.shape
    y, w_out, g_post = pend
    return out_proj(y.reshape(b * l, y.shape[-1]), w_out, x.reshape(b * l, d), g_post,
                    _largest_tile(b * l, 512, 8)).reshape(b, l, d)


def kernel(x_prompt, x_sample, cache_fox_k, cache_fox_v, cache_fox_logf, state_ssd, state_conv, norm_pre, norm_post, w_in_even, w_out_even, gmlp_ws, gmlp_bs, gmlp_gv, ssd_conv_w, ssd_conv_b, ssd_dt_bias, ssd_a_log, ssd_d, ssd_norm_g, w_in_odd, fox_b_forget, w_out_odd, fox_gq, fox_gk):
    xp, xs = x_prompt, x_sample
    bp = xp.shape[0]
    depth = norm_pre.shape[0]
    n_odd = depth // 2
    outs = {n: [] for n in ("lp", "sp", "cp", "ls", "ss", "cs", "gs")}
    kp = vp = ks = vs = None
    pend_p = pend_s = None
    cache = (cache_fox_k, cache_fox_v, cache_fox_logf)
    for i in range(depth):
        j = i // 2
        if i % 2 == 0:
            wts = _even_weights(w_in_even[j], w_out_even[j], gmlp_ws[j], gmlp_bs[j], gmlp_gv[j], ssd_conv_w[j],
                                ssd_conv_b[j], ssd_dt_bias[j], ssd_a_log[j], ssd_d[j], ssd_norm_g[j])
            xp, pend_p, sp, cp = _even_layer(xp, pend_p, norm_pre[i], norm_post[i], wts,
                                             jnp.zeros((bp, B_CONV - 1, B_CONV_DIM), F32),
                                             jnp.zeros((bp, B_HEADS, B_HEAD_DIM, B_STATE), F32), False)
            xs, pend_s, gs, ss, cs = _even_layer(xs, pend_s, norm_pre[i], norm_post[i], wts, state_conv[j],
                                                 state_ssd[j], True)
            for n, a in (("sp", sp), ("cp", cp), ("ss", ss), ("cs", cs), ("gs", gs)):
                outs[n].append(a)
        else:
            wts = _odd_weights(w_in_odd[j], fox_b_forget[j], w_out_odd[j], fox_gq[j], fox_gk[j])
            xp, pend_p, kp, vp, lp = _odd_layer(xp, pend_p, norm_pre[i], norm_post[i], wts, n_odd, j, kp, vp, None)
            xs, pend_s, ks, vs, ls = _odd_layer(xs, pend_s, norm_pre[i], norm_post[i], wts, n_odd, j, ks, vs,
                                                cache)
            outs["lp"].append(lp)
            outs["ls"].append(ls)
    xp, xs = _finish(xp, pend_p), _finish(xs, pend_s)
    st = {n: jnp.stack(a) for n, a in outs.items()}
    ks, vs = (a.reshape((n_odd,) + xs.shape[:2] + (C_HEADS, C_HEAD_DIM)) for a in (ks, vs))
    return (xp, xs, kp, vp, st["lp"], st["sp"], st["cp"], ks, vs, st["ls"], st["ss"], st["cs"], st["gs"])
```

```python
import functools

import numpy as np
import jax
import jax.numpy as jnp
from jax import lax
from jax.experimental import pallas as pl
from jax.experimental.pallas import tpu as pltpu

F32 = jnp.float32
BF16 = jnp.bfloat16
EPS = 1e-6
LANES = 128
VMEM_LIMIT = 56 * 1024 * 1024

D_MODEL = 1024
A_GROUPS = 4
A_GROUP_DIM = 256
A_WIDTH = A_GROUPS * A_GROUP_DIM
A_CHUNK = 128
SUB_CHUNK = 64
B_HEADS = 16
B_HEAD_DIM = 64
B_GROUPS = 2
B_STATE = 128
B_CONV = 4
B_WIDTH = 1024
B_CONV_DIM = B_WIDTH + 2 * B_GROUPS * B_STATE
CTX_ROW = 8
C_HEADS = 8
C_HEAD_DIM = 128
SQRT_HALF = 0.7071067811865476
LOG2E = 1.4426950408889634
AUG_PIECES = 3
AUG_HALF = AUG_PIECES * C_HEADS
MIXER_BLOCKS_PER_STEP = 8
PROJ_ROWS = 256
EVEN_PROJ_ROWS = 512
VT_KEYS = PROJ_ROWS


def _params(sem):
    return pltpu.CompilerParams(dimension_semantics=sem, vmem_limit_bytes=VMEM_LIMIT)


def _const_spec(shape):
    nd = len(shape)
    return pl.BlockSpec(shape, lambda *_: (0,) * nd, pipeline_mode=pl.Buffered(1))


def _dot(a, b):
    return jnp.dot(a, b, preferred_element_type=F32)


def _dot_nt(a, b):
    return lax.dot_general(a, b, (((1,), (1,)), ((), ())), preferred_element_type=F32)


def _split3(x):
    hi = x.astype(BF16)
    r = x - hi.astype(F32)
    mid = r.astype(BF16)
    lo = (r - mid.astype(F32)).astype(BF16)
    return hi, mid, lo


def _dot3_lhs(x, w):
    hi, mid, lo = _split3(x)
    return _dot(hi, w) + _dot(mid, w) + _dot(lo, w)


def _dot3_rhs(w, x):
    hi, mid, lo = _split3(x)
    return _dot(w, hi) + _dot(w, mid) + _dot(w, lo)


def _rms(x, g):
    return x * lax.rsqrt(jnp.mean(x * x, axis=-1, keepdims=True) + EPS) * g


def _gelu(x):
    return 0.5 * x * (1.0 + lax.erf(x * SQRT_HALF))


def _silu(x):
    hx = 0.5 * x
    return hx + hx * jnp.tanh(hx)


def _tril(n):
    row = lax.broadcasted_iota(jnp.int32, (n, n), 0)
    col = lax.broadcasted_iota(jnp.int32, (n, n), 1)
    return col <= row


def _even_proj_body(x_ref, g_ref, wu_ref, wv_ref, wza_ref, wzb_ref, wx_ref, wdt_ref, gv_ref, cw_ref, cb_ref,
                    dtb_ref, conv0_ref, *refs, tm, seg, emit_v):
    gate_ref, vb_ref, zs_ref, xc_ref, dt_ref, cfin_ref, prev_sc = refs[:2] + refs[-5:]
    j = pl.program_id(1)
    nseg = tm // seg
    tail = lambda s: slice(s * CTX_ROW + CTX_ROW - (B_CONV - 1), (s + 1) * CTX_ROW)

    @pl.when(j == 0)
    def _():
        prev_sc[...] = jnp.zeros_like(prev_sc)
        for s in range(nseg):
            prev_sc[tail(s), :] = conv0_ref[s]

    h = _rms(x_ref[...], g_ref[...]).astype(BF16)
    xbc = _dot(h, wx_ref[...])
    span = CTX_ROW + seg
    ext = jnp.concatenate([piece for s in range(nseg) for piece in
                           (prev_sc[s * CTX_ROW:(s + 1) * CTX_ROW, :], xbc[s * seg:(s + 1) * seg, :])], axis=0)
    acc = cb_ref[...] + xbc * cw_ref[B_CONV - 1:B_CONV, :]
    for back in range(1, B_CONV):
        rolled = pltpu.roll(ext, back, axis=0)
        shifted = jnp.concatenate([rolled[s * span + CTX_ROW:(s + 1) * span, :] for s in range(nseg)], axis=0)
        acc = acc + shifted * cw_ref[B_CONV - 1 - back:B_CONV - back, :]
    xc_ref[...] = _silu(acc)
    for s in range(nseg):
        prev_sc[s * CTX_ROW:(s + 1) * CTX_ROW, :] = xbc[(s + 1) * seg - CTX_ROW:(s + 1) * seg, :]
    gv = _gelu(_dot(h, wv_ref[...]))
    vn = jnp.concatenate(
        [_rms(gv[:, g * A_GROUP_DIM:(g + 1) * A_GROUP_DIM], gv_ref[g:g + 1, :]) for g in range(A_GROUPS)], axis=1)
    vb_ref[...] = vn.astype(BF16)
    if emit_v:
        refs[2][...] = vn
    gate_ref[...] = _silu(_dot(h, wza_ref[...])) * _gelu(_dot(h, wu_ref[...]))
    zs_ref[...] = _silu(_dot(h, wzb_ref[...]))
    dt_ref[...] = jax.nn.softplus(_dot(h, wdt_ref[...]) + dtb_ref[...])

    @pl.when(j == pl.num_programs(1) - 1)
    def _():
        for s in range(nseg):
            cfin_ref[s] = prev_sc[tail(s), :]


def even_proj(x, g, ws, gv, cw, cb, dtb, conv0, tm, seg, emit_v):
    b, l, d = x.shape
    nseg = tm // seg
    assert nseg == 1 or l == tm
    tok = lambda w: pl.BlockSpec((None, tm, w), lambda i, j: (i, j, 0))
    ctx = pl.BlockSpec((None, nseg, B_CONV - 1, B_CONV_DIM), lambda i, j: (i, 0, 0, 0))
    consts = [g] + list(ws) + [gv, cw, cb, dtb]
    act = lambda w, dt: jax.ShapeDtypeStruct((b, l, w), dt)
    v_spec, v_shape = ([tok(A_WIDTH)], [act(A_WIDTH, F32)]) if emit_v else ([], [])
    return pl.pallas_call(
        functools.partial(_even_proj_body, tm=tm, seg=seg, emit_v=emit_v),
        grid=(b, l // tm),
        in_specs=[tok(d)] + [_const_spec(c.shape) for c in consts] + [ctx],
        out_specs=[tok(A_WIDTH), tok(A_WIDTH)] + v_spec + [tok(B_WIDTH), tok(B_CONV_DIM), tok(LANES), ctx],
        out_shape=[act(A_WIDTH, F32), act(A_WIDTH, BF16)] + v_shape
        + [act(B_WIDTH, F32), act(B_CONV_DIM, F32), act(LANES, F32),
           jax.ShapeDtypeStruct((b, nseg, B_CONV - 1, B_CONV_DIM), F32)],
        scratch_shapes=[pltpu.VMEM((nseg * CTX_ROW, B_CONV_DIM), F32)],
        compiler_params=_params(("parallel", "arbitrary")),
        name="even_proj",
    )(x, *consts, conv0)


def _out_proj_body(y_ref, w_ref, x_ref, g_ref, o_ref):
    o = _dot(y_ref[...], w_ref[...])
    o_ref[...] = x_ref[...] + _rms(o, g_ref[...])


def out_proj(y, w, x, g, tm):
    t, d = x.shape
    kd = y.shape[1]
    return pl.pallas_call(
        _out_proj_body,
        grid=(t // tm,),
        in_specs=[pl.BlockSpec((tm, kd), lambda i: (i, 0)), _const_spec(w.shape),
                  pl.BlockSpec((tm, d), lambda i: (i, 0)), _const_spec((1, d))],
        out_specs=pl.BlockSpec((tm, d), lambda i: (i, 0)),
        out_shape=jax.ShapeDtypeStruct((t, d), F32),
        compiler_params=_params(("parallel",)),
        name="out_proj",
    )(y, w, x, g)


def _even_mixer_body(gate_ref, vb_ref, zs_ref, xc_ref, dt_ref, ws_ref, bs_ref, alog_ref, dskip_ref, gssd_ref,
                     e_ref, s0_ref, y_ref, sfin_ref, st_sc, *, blk, nsub):
    j = pl.program_id(1)

    @pl.when(j == 0)
    def _():
        st_sc[...] = s0_ref[...].T

    for sub in range(nsub):
        _mixer_block(slice(sub * blk, (sub + 1) * blk), blk, gate_ref, vb_ref, zs_ref, xc_ref, dt_ref, ws_ref,
                     bs_ref, alog_ref, dskip_ref, gssd_ref, e_ref, y_ref, st_sc)

    @pl.when(j == pl.num_programs(1) - 1)
    def _():
        sfin_ref[...] = st_sc[...].T


def _mixer_block(rs, blk, gate_ref, vb_ref, zs_ref, xc_ref, dt_ref, ws_ref, bs_ref, alog_ref, dskip_ref, gssd_ref,
                 e_ref, y_ref, st_sc):
    row = lax.broadcasted_iota(jnp.int32, (blk, blk), 0)
    col = lax.broadcasted_iota(jnp.int32, (blk, blk), 1)
    causal = col <= row

    chunk_causal = (col // SUB_CHUNK) <= (row // SUB_CHUNK)
    s_parts = []
    for g in range(A_GROUPS):
        wg = jnp.where(chunk_causal, ws_ref[g], 0.0).astype(BF16)
        sg = _dot(wg, vb_ref[rs, g * A_GROUP_DIM:(g + 1) * A_GROUP_DIM])
        s_parts.append(sg + bs_ref[:, g:g + 1])
    ya = gate_ref[rs, :] * jnp.concatenate(s_parts, axis=1)

    xs = xc_ref[rs, :B_WIDTH]
    bm = xc_ref[rs, B_WIDTH:B_WIDTH + B_GROUPS * B_STATE]
    cm = xc_ref[rs, B_WIDTH + B_GROUPS * B_STATE:]
    lane = lax.broadcasted_iota(jnp.int32, (1, LANES), 1)
    dt = dt_ref[rs, :]
    a = jnp.where(lane < B_HEADS, -jnp.exp(alog_ref[...]), 0.0)
    acs = _dot3_rhs(causal.astype(BF16), dt * a)
    acs_t = acs.T
    e = e_ref[...]
    dt_e = _dot3_lhs(dt, e)
    acs_e = _dot3_lhs(acs, e)
    expacs_e = jnp.exp(acs_e)
    toend_e = jnp.exp(acs_e[blk - 1:blk, :] - acs_e)
    blkdec_e = expacs_e[blk - 1:blk, :]
    dtx = dt_e * xs
    wds = (toend_e * dtx).astype(BF16)
    dtx_b = dtx.astype(BF16)
    lane_hp = lax.broadcasted_iota(jnp.int32, (blk, LANES), 1)
    gw = (B_HEADS // B_GROUPS) * B_HEAD_DIM
    ys_parts, yi_parts = [], []
    for g in range(B_GROUPS):
        bm_g = bm[:, g * B_STATE:(g + 1) * B_STATE]
        cm_g = cm[:, g * B_STATE:(g + 1) * B_STATE].astype(BF16)
        cb = _dot_nt(cm_g, bm_g.astype(BF16))
        st_g = st_sc[:, g * gw:(g + 1) * gw]
        ds = _dot(bm_g.T.astype(BF16), wds[:, g * gw:(g + 1) * gw])
        st_sc[:, g * gw:(g + 1) * gw] = blkdec_e[:, g * gw:(g + 1) * gw] * st_g + ds
        ys_parts.append(_dot(cm_g, st_g.astype(BF16)))
        for hp in range(gw // LANES):
            c0 = g * gw + hp * LANES
            rhs = dtx_b[:, c0:c0 + LANES]
            pair = None
            for half in range(2):
                h = c0 // B_HEAD_DIM + half
                seg = acs[:, h:h + 1] - acs_t[h:h + 1, :]
                decay = jnp.exp(jnp.where(causal, seg, -jnp.inf))
                m_h = (cb * decay).astype(BF16)
                keep = (lane_hp < B_HEAD_DIM) if half == 0 else (lane_hp >= B_HEAD_DIM)
                part = _dot(m_h, jnp.where(keep, rhs, jnp.zeros_like(rhs)))
                pair = part if pair is None else pair + part
            yi_parts.append(pair)
    y = (jnp.concatenate(yi_parts, axis=1) + jnp.concatenate(ys_parts, axis=1) * expacs_e
         + dskip_ref[...] * xs)
    y = y * zs_ref[rs, :]
    half_w = B_WIDTH // B_GROUPS
    yb = jnp.concatenate(
        [_rms(y[:, g * half_w:(g + 1) * half_w], gssd_ref[:, g * half_w:(g + 1) * half_w])
         for g in range(B_GROUPS)], axis=1)
    y_ref[rs, :] = jnp.concatenate([ya, yb], axis=1).astype(y_ref.dtype)


def even_mixer(gate, vb, zs, xc, dt, ws, bs_t, alog, dskip_e, gssd, e_mat, s0, blk):
    b, l, _ = gate.shape
    nsub = MIXER_BLOCKS_PER_STEP if l % (MIXER_BLOCKS_PER_STEP * blk) == 0 else 1
    tok = lambda w: pl.BlockSpec((None, nsub * blk, w), lambda i, j: (i, j, 0))
    hp = B_HEADS * B_HEAD_DIM
    state = pl.BlockSpec((None, hp, B_STATE), lambda i, j: (i, 0, 0))
    consts = [ws, bs_t, alog, dskip_e, gssd, e_mat]
    return pl.pallas_call(
        functools.partial(_even_mixer_body, blk=blk, nsub=nsub),
        grid=(b, l // (nsub * blk)),
        in_specs=[tok(A_WIDTH), tok(A_WIDTH), tok(B_WIDTH), tok(B_CONV_DIM), tok(LANES)]
        + [_const_spec(c.shape) for c in consts] + [state],
        out_specs=[tok(A_WIDTH + B_WIDTH), state],
        out_shape=[jax.ShapeDtypeStruct((b, l, A_WIDTH + B_WIDTH), BF16), jax.ShapeDtypeStruct((b, hp, B_STATE), F32)],
        scratch_shapes=[pltpu.VMEM((B_STATE, hp), F32)],
        compiler_params=_params(("parallel", "arbitrary")),
        name="even_mixer",
    )(gate, vb, zs, xc, dt, *consts, s0)


def _reset_at_stream_start(carry_sc):
    @pl.when(pl.program_id(1) == 0)
    def _():
        carry_sc[...] = jnp.zeros_like(carry_sc)


def _forget_cumsum(lf, carry_sc):
    n = lf.shape[0]
    lane = lax.broadcasted_iota(jnp.int32, (1, LANES), 1)
    lf = jnp.where(lane < C_HEADS, lf, 0.0)
    chunk = _largest_tile(n, 256, 8)
    tri = _tril(chunk).astype(BF16)
    carry, parts = carry_sc[...], []
    for c in range(n // chunk):
        part = _dot3_rhs(tri, lf[c * chunk:(c + 1) * chunk, :]) + carry
        carry = part[chunk - 1:chunk, :]
        parts.append(part)
    carry_sc[...] = carry
    return jnp.concatenate(parts, axis=0)


def _forget_lanes(fc):
    lane = lax.broadcasted_iota(jnp.int32, (1, LANES), 1)
    hi, mid, lo = _split3(fc * LOG2E)
    pieces = (hi.astype(F32) + pltpu.roll(mid.astype(F32), C_HEADS, axis=1)
              + pltpu.roll(lo.astype(F32), 2 * C_HEADS, axis=1))
    q_side = pieces + ((lane >= AUG_HALF) & (lane < 2 * AUG_HALF)).astype(F32)
    k_side = (lane < AUG_HALF).astype(F32) - pltpu.roll(pieces, AUG_HALF, axis=1)
    return q_side.astype(BF16), k_side.astype(BF16)


def _head_lanes(k_side, head):
    lane = lax.broadcasted_iota(jnp.int32, (1, LANES), 1)
    return jnp.where((lane % C_HEADS == head) & (lane < 2 * AUG_HALF), k_side, jnp.zeros_like(k_side))


def _fox_proj_body(x_ref, g_ref, wq_ref, wk_ref, wv_ref, wz_ref, wf_ref, gq_ref, gk_ref, bf_ref, *refs,
                   slot, first, with_aug):
    if not first:
        refs = refs[2:]
    q_ref, kf_ref, kb_ref, vf_ref, vb_ref, z_ref, lf_ref = refs[:7]
    tm = x_ref.shape[0]
    if with_aug:
        qa_ref, ka_ref, carry_sc = refs[7:]
        _reset_at_stream_start(carry_sc)
    if first:
        for s in range(kf_ref.shape[0]):
            if s != slot:
                kf_ref[s] = jnp.zeros(kf_ref.shape[1:], F32)
                vf_ref[s] = jnp.zeros(vf_ref.shape[1:], F32)
        kf_ref, vf_ref = kf_ref.at[slot], vf_ref.at[slot]
    h = _rms(x_ref[...], g_ref[...]).astype(BF16)
    q = _dot(h, wq_ref[...])
    k = _dot(h, wk_ref[...])
    v = _dot(h, wv_ref[...])
    q_scale = C_HEAD_DIM ** -0.5 * LOG2E
    kn = []
    for hd in range(C_HEADS):
        sl = slice(hd * C_HEAD_DIM, (hd + 1) * C_HEAD_DIM)
        q_ref[:, sl] = (_rms(q[:, sl], gq_ref[...]) * q_scale).astype(BF16)
        kn.append(_rms(k[:, sl], gk_ref[...]))
    kn = jnp.concatenate(kn, axis=1)
    kf_ref[...] = kn.reshape(tm, C_HEADS, C_HEAD_DIM)
    kb_ref[...] = kn.astype(BF16)
    vf_ref[...] = v.reshape(tm, C_HEADS, C_HEAD_DIM)
    vb_ref[...] = (v.T if with_aug else v).astype(BF16)
    z_ref[...] = _dot(h, wz_ref[...])
    lf = jax.nn.log_sigmoid(_dot(h, wf_ref[...]) + bf_ref[...])
    lf_ref[...] = lf
    if with_aug:
        qa_ref[...], ka_ref[...] = _forget_lanes(_forget_cumsum(lf, carry_sc))


def fox_proj(x, g, wq, wk, wv, wz, wf, gq, gk, bf, tm, n_slots, slot, prev_k, prev_v, with_aug):
    b, l, d = x.shape
    tok = lambda w: pl.BlockSpec((None, tm, w), lambda i, j: (i, j, 0))
    act = lambda w, dt: jax.ShapeDtypeStruct((b, l, w), dt)
    ins = [g, wq, wk, wv, wz, wf, gq, gk, bf]
    first = prev_k is None
    stacked = jax.ShapeDtypeStruct((n_slots, b, l, C_HEADS, C_HEAD_DIM), F32)
    if first:
        st_spec = pl.BlockSpec((n_slots, None, tm, C_HEADS, C_HEAD_DIM), lambda i, j: (0, i, j, 0, 0))
        extra, extra_specs, aliases = [], [], {}
    else:
        st_spec = pl.BlockSpec((None, None, tm, C_HEADS, C_HEAD_DIM), lambda i, j: (slot, i, j, 0, 0))
        extra, extra_specs = [prev_k, prev_v], [pl.BlockSpec(memory_space=pl.ANY)] * 2
        aliases = {1 + len(ins): 1, 2 + len(ins): 3}
    w = C_HEADS * C_HEAD_DIM
    aug_specs, aug_shapes, scratch = [], [], []
    v_spec, v_shape = tok(w), act(w, BF16)
    if with_aug:
        assert tm == VT_KEYS
        aug_specs, aug_shapes = [tok(LANES)] * 2, [act(LANES, BF16)] * 2
        scratch = [pltpu.VMEM((1, LANES), F32)]
        v_spec = pl.BlockSpec((None, None, w, tm), lambda i, j: (i, j, 0, 0))
        v_shape = jax.ShapeDtypeStruct((b, l // tm, w, tm), BF16)
    return pl.pallas_call(
        functools.partial(_fox_proj_body, slot=slot, first=first, with_aug=with_aug),
        grid=(b, l // tm),
        in_specs=[tok(d)] + [_const_spec(a.shape) for a in ins] + extra_specs,
        out_specs=[tok(w), st_spec, tok(w), st_spec, v_spec, tok(w), tok(LANES)] + aug_specs,
        out_shape=[act(w, BF16), stacked, act(w, BF16), stacked, v_shape, act(w, F32), act(LANES, F32)]
        + aug_shapes,
        input_output_aliases=aliases,
        scratch_shapes=scratch,
        compiler_params=_params(("parallel", "arbitrary")),
        name="fox_proj",
    )(x, *ins, *extra)


def _fox_aug_body(lf_ref, qa_ref, ka_ref, carry_sc):
    _reset_at_stream_start(carry_sc)
    qa_ref[...], ka_ref[...] = _forget_lanes(_forget_cumsum(lf_ref[...], carry_sc))


def fox_aug(lf, ts):
    b, s, _ = lf.shape
    return pl.pallas_call(
        _fox_aug_body,
        grid=(b, s // ts),
        in_specs=[pl.BlockSpec((None, ts, LANES), lambda i, j: (i, j, 0))],
        out_specs=[pl.BlockSpec((None, ts, LANES), lambda i, j: (i, j, 0))] * 2,
        out_shape=[jax.ShapeDtypeStruct((b, s, LANES), BF16)] * 2,
        scratch_shapes=[pltpu.VMEM((1, LANES), F32)],
        compiler_params=_params(("parallel", "arbitrary")),
        name="fox_aug",
    )(lf)


ATTN_ROWS = 1024
ATTN_CHAINS = 2


def _softmax_step(qc, carry, kc, vts, diagonal):
    m, l, acc = carry
    st = _dot_nt(kc, qc)
    if diagonal:
        row = lax.broadcasted_iota(jnp.int32, st.shape, 0)
        col = lax.broadcasted_iota(jnp.int32, st.shape, 1)
        st = jnp.where(row <= col, st, -jnp.inf)
    m_new = jnp.maximum(m, jnp.max(st, axis=0, keepdims=True))
    alpha = jnp.exp2(m - m_new)
    p = jnp.exp2(st - m_new)
    l = alpha * l + jnp.sum(p, axis=0, keepdims=True)
    pb = p.astype(BF16)
    pv = _dot(vts[0], pb[:VT_KEYS, :])
    for u in range(1, len(vts)):
        pv = pv + _dot(vts[u], pb[u * VT_KEYS:(u + 1) * VT_KEYS, :])
    return m_new, l, alpha * acc + pv


def _fox_attn_body(q_ref, qa_ref, k_ref, ka_ref, vt_ref, z_ref, o_ref, *, rows, chains, part):
    qi = pl.program_id(2)
    tq = rows * chains
    tiles = lambda n: n // VT_KEYS

    def queries(r0, n):
        return jnp.concatenate([q_ref[r0:r0 + n, :], qa_ref[r0:r0 + n, :]], axis=1)

    def keys(r0, n):
        return jnp.concatenate([k_ref[pl.ds(r0, n), :], _head_lanes(ka_ref[pl.ds(r0, n), :], pl.program_id(1))],
                               axis=1)

    qcs = [queries(c * rows, rows) for c in range(chains)]

    def below_diagonal(j, carries):
        kc = keys(pl.multiple_of(j * tq, tq), tq)
        vts = [vt_ref[j * tiles(tq) + u] for u in range(tiles(tq))]
        return tuple(_softmax_step(qcs[c], carries[c], kc, vts, False) for c in range(chains))

    init = tuple((jnp.full((1, rows), -jnp.inf, F32), jnp.zeros((1, rows), F32), jnp.zeros((C_HEAD_DIM, rows), F32))
                 for _ in range(chains))
    carries = lax.fori_loop(0, qi, below_diagonal, init)

    base = pl.multiple_of(qi * tq, tq)
    kc = keys(base, tq)
    vts = [vt_ref[qi * tiles(tq) + u] for u in range(tiles(tq))]
    for i in range(tq // part):
        c, o = divmod(i * part, rows)
        carry = tuple(a[:, o:o + part] for a in carries[c])
        qc = queries(i * part, part)
        if i > 0:
            carry = _softmax_step(qc, carry, kc[:i * part, :], vts[:tiles(i * part)], False)
        _, l, acc = _softmax_step(qc, carry, kc[i * part:(i + 1) * part, :],
                                  vts[tiles(i * part):tiles((i + 1) * part)], True)
        sl = slice(i * part, (i + 1) * part)
        o_ref[sl, :] = (_silu(z_ref[sl, :]) * (acc / l).T).astype(o_ref.dtype)


def fox_attn(q, qa, k, ka, vt, z):
    b, l, w = q.shape
    chains = ATTN_CHAINS if l % (ATTN_ROWS * ATTN_CHAINS) == 0 else 1
    rows = ATTN_ROWS if l % (ATTN_ROWS * chains) == 0 else l // chains
    tq = rows * chains
    part = rows
    qspec = pl.BlockSpec((None, tq, C_HEAD_DIM), lambda i, h, j: (i, j, h))
    qaspec = pl.BlockSpec((None, tq, LANES), lambda i, h, j: (i, j, 0))
    kspec = pl.BlockSpec((None, l, C_HEAD_DIM), lambda i, h, j: (i, 0, h))
    kaspec = pl.BlockSpec((None, l, LANES), lambda i, h, j: (i, 0, 0))
    vspec = pl.BlockSpec((None, l // VT_KEYS, C_HEAD_DIM, VT_KEYS), lambda i, h, j: (i, 0, h, 0))
    return pl.pallas_call(
        functools.partial(_fox_attn_body, rows=rows, chains=chains, part=part),
        grid=(b, C_HEADS, l // tq),
        in_specs=[qspec, qaspec, kspec, kaspec, vspec, qspec],
        out_specs=qspec,
        out_shape=jax.ShapeDtypeStruct((b, l, w), BF16),
        compiler_params=_params(("parallel", "parallel", "arbitrary")),
        name="fox_attn",
    )(q, qa, k, ka, vt, z)


def _fox_attn_sample_body(q_ref, qa_ref, kp_ref, kap_ref, vp_ref, kn_ref, kan_ref, vn_ref, z_ref, o_ref, *, p0):
    n = q_ref.shape[0]
    for hd in range(C_HEADS):
        sl = slice(hd * C_HEAD_DIM, (hd + 1) * C_HEAD_DIM)
        head_rows = pl.ds(hd, p0, stride=C_HEADS)
        qc = jnp.concatenate([q_ref[:, sl], qa_ref[...]], axis=1)
        s_past = _dot_nt(qc, jnp.concatenate([kp_ref[head_rows, :].astype(BF16), _head_lanes(kap_ref[...], hd)],
                                             axis=1))
        s_new = _dot_nt(qc, jnp.concatenate([kn_ref[:, sl], _head_lanes(kan_ref[...], hd)], axis=1))
        s_new = jnp.where(_tril(n), s_new, -jnp.inf)
        m = jnp.maximum(jnp.max(s_past, axis=-1, keepdims=True), jnp.max(s_new, axis=-1, keepdims=True))
        p_past = jnp.exp2(s_past - m)
        p_new = jnp.exp2(s_new - m)
        l = jnp.sum(p_past, axis=-1, keepdims=True) + jnp.sum(p_new, axis=-1, keepdims=True)
        acc = (_dot(p_past.astype(BF16), vp_ref[head_rows, :].astype(BF16))
               + _dot(p_new.astype(BF16), vn_ref[:, sl]))
        o_ref[:, sl] = (_silu(z_ref[:, sl]) * (acc / l)).astype(o_ref.dtype)


def fox_attn_sample(q, qa_all, cache_k, ka_all, cache_v, layer, k_new, v_new, z):
    b, n, w = q.shape
    p0 = cache_k.shape[2] // C_HEADS
    new = pl.BlockSpec((None, n, w), lambda i: (i, 0, 0))
    new_of_all = pl.BlockSpec((None, n, LANES), lambda i: (i, p0 // n, 0))
    past_aug = pl.BlockSpec((None, p0, LANES), lambda i: (i, 0, 0))
    past = pl.BlockSpec((None, None, p0 * C_HEADS, C_HEAD_DIM), lambda i: (layer, i, 0, 0))
    return pl.pallas_call(
        functools.partial(_fox_attn_sample_body, p0=p0),
        grid=(b,),
        in_specs=[new, new_of_all, past, past_aug, past, new, new_of_all, new, new],
        out_specs=new,
        out_shape=jax.ShapeDtypeStruct((b, n, w), BF16),
        compiler_params=_params(("parallel",)),
        name="fox_attn_sample",
    )(q, qa_all, cache_k, ka_all, cache_v, k_new, ka_all, v_new, z)


def _largest_tile(n, cap, mult):
    t = min(n, cap)
    while n % t or t % mult:
        t -= 1
    return t


def _pad_cols(a, width):
    return jnp.pad(a, ((0, 0), (0, width - a.shape[1])))


def _even_weights(w_in, w_out, ws, bs, gv, conv_w, conv_b, dt_bias, a_log, d_skip, g_ssd):
    cuts = (A_WIDTH, 2 * A_WIDTH, 3 * A_WIDTH, 3 * A_WIDTH + B_WIDTH, 3 * A_WIDTH + B_WIDTH + B_CONV_DIM)
    wu, wv, wza, wzb, wx, wdt = jnp.split(w_in, cuts, axis=1)
    e_mat = np.zeros((LANES, B_HEADS * B_HEAD_DIM), np.float32)
    for h in range(B_HEADS):
        e_mat[h, h * B_HEAD_DIM:(h + 1) * B_HEAD_DIM] = 1.0
    return dict(
        w_in=[p.astype(BF16) for p in (wu, wv, wza, wzb, wx, _pad_cols(wdt, LANES))], w_out=w_out.astype(BF16),
        ws=ws, bs=bs, gv=gv, cw=conv_w.T, cb=conv_b[None, :], dtb=_pad_cols(dt_bias[None, :], LANES),
        alog=_pad_cols(a_log[None, :], LANES), dskip_e=jnp.repeat(d_skip, B_HEAD_DIM)[None, :],
        gssd=g_ssd[None, :], e_mat=jnp.asarray(e_mat, BF16))


def _even_layer(x, g_pre, g_post, wts, conv0, s0, emit_v):
    b, l, d = x.shape
    t = b * l
    if l >= PROJ_ROWS:
        xin, tm, seg = x, _largest_tile(l, EVEN_PROJ_ROWS, 16), _largest_tile(l, EVEN_PROJ_ROWS, 16)
    else:
        xin, tm, seg = x.reshape(1, t, d), t, l
    outs = even_proj(xin, g_pre[None, :], wts["w_in"], wts["gv"], wts["cw"], wts["cb"], wts["dtb"],
                     conv0.reshape(xin.shape[0], tm // seg, B_CONV - 1, B_CONV_DIM), tm, seg, emit_v)
    gate, vb, *vrows, zs, xc, dt = (a.reshape(b, l, a.shape[2]) for a in outs[:-1])
    cfin = outs[-1].reshape(b, B_CONV - 1, B_CONV_DIM)
    blk = min(l, A_CHUNK)
    ycat, sfin = even_mixer(gate, vb, zs, xc, dt, wts["ws"][:, :blk, :blk], wts["bs"][:, :blk].T, wts["alog"],
                            wts["dskip_e"], wts["gssd"], wts["e_mat"],
                            s0.reshape(b, B_HEADS * B_HEAD_DIM, B_STATE), blk)
    xn = out_proj(ycat.reshape(t, ycat.shape[2]), wts["w_out"], x.reshape(t, d), g_post[None, :],
                  _largest_tile(t, 512, 8))
    return (xn.reshape(b, l, d), *vrows, sfin.reshape(b, B_HEADS, B_HEAD_DIM, B_STATE), cfin)


def _odd_weights(w_in, b_f, w_out, gq, gk):
    w = C_HEADS * C_HEAD_DIM
    wq, wk, wv, wz, wf = jnp.split(w_in, (w, 2 * w, 3 * w, 4 * w), axis=1)
    return dict(wq=wq.astype(BF16), wk=wk.astype(BF16), wv=wv.astype(BF16), wz=wz.astype(BF16),
                wf=_pad_cols(wf, LANES).astype(BF16), bf=_pad_cols(b_f[None, :], LANES),
                w_out=w_out.astype(BF16), gq=gq[None, :], gk=gk[None, :])


def _odd_layer(x, g_pre, g_post, wts, n_slots, slot, prev_k, prev_v, cache):
    b, l, d = x.shape
    t = b * l
    prompt = cache is None
    xin = x if prompt else x.reshape(1, t, d)
    q, kst, kb, vst, vb, z, lf, *aug = fox_proj(
        xin, g_pre[None, :], wts["wq"], wts["wk"], wts["wv"], wts["wz"], wts["wf"], wts["gq"], wts["gk"],
        wts["bf"], _largest_tile(xin.shape[1], PROJ_ROWS, 16), n_slots, slot, prev_k, prev_v, prompt)
    if prompt:
        yc = fox_attn(q, aug[0], kb, aug[1], vb, z)
    else:
        q, kb, vb, z, lf = (a.reshape(b, l, a.shape[2]) for a in (q, kb, vb, z, lf))
        cache_k, cache_v, cache_logf = cache
        p0 = cache_k.shape[2]
        past_lf = _pad_cols(cache_logf[slot].reshape(b * p0, C_HEADS), LANES).reshape(b, p0, LANES)
        qa, ka = fox_aug(jnp.concatenate([past_lf, lf], axis=1), _largest_tile(p0 + l, 1040, 16))
        cache_rows = lambda c: c.reshape(c.shape[0], b, p0 * C_HEADS, C_HEAD_DIM)
        yc = fox_attn_sample(q, qa, cache_rows(cache_k), ka, cache_rows(cache_v), slot, kb, vb, z)
    xn = out_proj(yc.reshape(t, d), wts["w_out"], x.reshape(t, d), g_post[None, :], _largest_tile(t, 512, 8))
    return xn.reshape(b, l, d), kst, vst, lf[:, :, :C_HEADS]


def kernel(x_prompt, x_sample, cache_fox_k, cache_fox_v, cache_fox_logf, state_ssd, state_conv, norm_pre, norm_post, w_in_even, w_out_even, gmlp_ws, gmlp_bs, gmlp_gv, ssd_conv_w, ssd_conv_b, ssd_dt_bias, ssd_a_log, ssd_d, ssd_norm_g, w_in_odd, fox_b_forget, w_out_odd, fox_gq, fox_gk):
    xp, xs = x_prompt, x_sample
    bp = xp.shape[0]
    depth = norm_pre.shape[0]
    n_odd = depth // 2
    outs = {n: [] for n in ("lp", "sp", "cp", "ls", "ss", "cs", "gs")}
    kp = vp = ks = vs = None
    cache = (cache_fox_k, cache_fox_v, cache_fox_logf)
    for i in range(depth):
        j = i // 2
        if i % 2 == 0:
            wts = _even_weights(w_in_even[j], w_out_even[j], gmlp_ws[j], gmlp_bs[j], gmlp_gv[j], ssd_conv_w[j],
                                ssd_conv_b[j], ssd_dt_bias[j], ssd_a_log[j], ssd_d[j], ssd_norm_g[j])
            xp, sp, cp = _even_layer(xp, norm_pre[i], norm_post[i], wts,
                                     jnp.zeros((bp, B_CONV - 1, B_CONV_DIM), F32),
                                     jnp.zeros((bp, B_HEADS, B_HEAD_DIM, B_STATE), F32), False)
            xs, gs, ss, cs = _even_layer(xs, norm_pre[i], norm_post[i], wts, state_conv[j], state_ssd[j], True)
            for n, a in (("sp", sp), ("cp", cp), ("ss", ss), ("cs", cs), ("gs", gs)):
                outs[n].append(a)
        else:
            wts = _odd_weights(w_in_odd[j], fox_b_forget[j], w_out_odd[j], fox_gq[j], fox_gk[j])
            xp, kp, vp, lp = _odd_layer(xp, norm_pre[i], norm_post[i], wts, n_odd, j, kp, vp, None)
            xs, ks, vs, ls = _odd_layer(xs, norm_pre[i], norm_post[i], wts, n_odd, j, ks, vs, cache)
            outs["lp"].append(lp)
            outs["ls"].append(ls)
    st = {n: jnp.stack(a) for n, a in outs.items()}
    ks, vs = (a.reshape((n_odd,) + xs.shape[:2] + (C_HEADS, C_HEAD_DIM)) for a in (ks, vs))
    return (xp, xs, kp, vp, st["lp"], st["sp"], st["cp"], ks, vs, st["ls"], st["ss"], st["cs"], st["gs"])
```

```python
import functools

import numpy as np
import jax
import jax.numpy as jnp
from jax import lax
from jax.experimental import pallas as pl
from jax.experimental.pallas import tpu as pltpu

F32 = jnp.float32
BF16 = jnp.bfloat16
EPS = 1e-6
LANES = 128
VMEM_LIMIT = 56 * 1024 * 1024

D_MODEL = 1024
A_GROUPS = 4
A_GROUP_DIM = 256
A_WIDTH = A_GROUPS * A_GROUP_DIM
A_CHUNK = 128
SUB_CHUNK = 64
B_HEADS = 16
B_HEAD_DIM = 64
B_GROUPS = 2
B_STATE = 128
B_CONV = 4
B_WIDTH = 1024
B_CONV_DIM = B_WIDTH + 2 * B_GROUPS * B_STATE
CTX_ROW = 8
C_HEADS = 8
C_HEAD_DIM = 128
SQRT_HALF = 0.7071067811865476
LOG2E = 1.4426950408889634
AUG_PIECES = 3
AUG_HALF = AUG_PIECES * C_HEADS
MIXER_BLOCKS_PER_STEP = 8
PROJ_ROWS = 512
EVEN_PROJ_ROWS = PROJ_ROWS
OUT_PROJ_ROWS = 1024
VT_KEYS = PROJ_ROWS


def _params(sem):
    return pltpu.CompilerParams(dimension_semantics=sem, vmem_limit_bytes=VMEM_LIMIT)


def _const_spec(shape):
    nd = len(shape)
    return pl.BlockSpec(shape, lambda *_: (0,) * nd, pipeline_mode=pl.Buffered(1))


def _dot(a, b):
    return jnp.dot(a, b, preferred_element_type=F32)


def _dot_nt(a, b):
    return lax.dot_general(a, b, (((1,), (1,)), ((), ())), preferred_element_type=F32)


def _split3(x):
    hi = x.astype(BF16)
    r = x - hi.astype(F32)
    mid = r.astype(BF16)
    lo = (r - mid.astype(F32)).astype(BF16)
    return hi, mid, lo


def _dot3_lhs(x, w):
    hi, mid, lo = _split3(x)
    return _dot(hi, w) + _dot(mid, w) + _dot(lo, w)


def _dot3_rhs(w, x):
    hi, mid, lo = _split3(x)
    return _dot(w, hi) + _dot(w, mid) + _dot(w, lo)


def _rms(x, g):
    return x * lax.rsqrt(jnp.mean(x * x, axis=-1, keepdims=True) + EPS) * g


def _gelu(x):
    return 0.5 * x * (1.0 + lax.erf(x * SQRT_HALF))


def _silu(x):
    hx = 0.5 * x
    return hx + hx * jnp.tanh(hx)


def _tril(n):
    row = lax.broadcasted_iota(jnp.int32, (n, n), 0)
    col = lax.broadcasted_iota(jnp.int32, (n, n), 1)
    return col <= row


def _even_proj_body(x_ref, g_ref, wu_ref, wv_ref, wza_ref, wzb_ref, wx_ref, wdt_ref, gv_ref, cw_ref, cb_ref,
                    dtb_ref, conv0_ref, *refs, tm, seg, emit_v):
    gate_ref, vb_ref, zs_ref, xc_ref, dt_ref, cfin_ref, prev_sc = refs[:2] + refs[-5:]
    j = pl.program_id(1)
    nseg = tm // seg
    tail = lambda s: slice(s * CTX_ROW + CTX_ROW - (B_CONV - 1), (s + 1) * CTX_ROW)

    @pl.when(j == 0)
    def _():
        prev_sc[...] = jnp.zeros_like(prev_sc)
        for s in range(nseg):
            prev_sc[tail(s), :] = conv0_ref[s]

    h = _rms(x_ref[...], g_ref[...]).astype(BF16)
    xbc = _dot(h, wx_ref[...])
    span = CTX_ROW + seg
    ext = jnp.concatenate([piece for s in range(nseg) for piece in
                           (prev_sc[s * CTX_ROW:(s + 1) * CTX_ROW, :], xbc[s * seg:(s + 1) * seg, :])], axis=0)
    acc = cb_ref[...] + xbc * cw_ref[B_CONV - 1:B_CONV, :]
    for back in range(1, B_CONV):
        rolled = pltpu.roll(ext, back, axis=0)
        shifted = jnp.concatenate([rolled[s * span + CTX_ROW:(s + 1) * span, :] for s in range(nseg)], axis=0)
        acc = acc + shifted * cw_ref[B_CONV - 1 - back:B_CONV - back, :]
    xc_ref[...] = _silu(acc)
    for s in range(nseg):
        prev_sc[s * CTX_ROW:(s + 1) * CTX_ROW, :] = xbc[(s + 1) * seg - CTX_ROW:(s + 1) * seg, :]
    gv = _gelu(_dot(h, wv_ref[...]))
    vn = jnp.concatenate(
        [_rms(gv[:, g * A_GROUP_DIM:(g + 1) * A_GROUP_DIM], gv_ref[g:g + 1, :]) for g in range(A_GROUPS)], axis=1)
    vb_ref[...] = vn.astype(BF16)
    if emit_v:
        refs[2][...] = vn
    gate_ref[...] = _silu(_dot(h, wza_ref[...])) * _gelu(_dot(h, wu_ref[...]))
    zs_ref[...] = _silu(_dot(h, wzb_ref[...]))
    dt_ref[...] = jax.nn.softplus(_dot(h, wdt_ref[...]) + dtb_ref[...])

    @pl.when(j == pl.num_programs(1) - 1)
    def _():
        for s in range(nseg):
            cfin_ref[s] = prev_sc[tail(s), :]


def even_proj(x, g, ws, gv, cw, cb, dtb, conv0, tm, seg, emit_v):
    b, l, d = x.shape
    nseg = tm // seg
    assert nseg == 1 or l == tm
    tok = lambda w: pl.BlockSpec((None, tm, w), lambda i, j: (i, j, 0))
    ctx = pl.BlockSpec((None, nseg, B_CONV - 1, B_CONV_DIM), lambda i, j: (i, 0, 0, 0))
    consts = [g] + list(ws) + [gv, cw, cb, dtb]
    act = lambda w, dt: jax.ShapeDtypeStruct((b, l, w), dt)
    v_spec, v_shape = ([tok(A_WIDTH)], [act(A_WIDTH, F32)]) if emit_v else ([], [])
    return pl.pallas_call(
        functools.partial(_even_proj_body, tm=tm, seg=seg, emit_v=emit_v),
        grid=(b, l // tm),
        in_specs=[tok(d)] + [_const_spec(c.shape) for c in consts] + [ctx],
        out_specs=[tok(A_WIDTH), tok(A_WIDTH)] + v_spec + [tok(B_WIDTH), tok(B_CONV_DIM), tok(LANES), ctx],
        out_shape=[act(A_WIDTH, F32), act(A_WIDTH, BF16)] + v_shape
        + [act(B_WIDTH, F32), act(B_CONV_DIM, F32), act(LANES, F32),
           jax.ShapeDtypeStruct((b, nseg, B_CONV - 1, B_CONV_DIM), F32)],
        scratch_shapes=[pltpu.VMEM((nseg * CTX_ROW, B_CONV_DIM), F32)],
        compiler_params=_params(("parallel", "arbitrary")),
        name="even_proj",
    )(x, *consts, conv0)


def _out_proj_body(y_ref, w_ref, x_ref, g_ref, o_ref):
    o = _dot(y_ref[...], w_ref[...])
    o_ref[...] = x_ref[...] + _rms(o, g_ref[...])


def out_proj(y, w, x, g, tm):
    t, d = x.shape
    kd = y.shape[1]
    return pl.pallas_call(
        _out_proj_body,
        grid=(t // tm,),
        in_specs=[pl.BlockSpec((tm, kd), lambda i: (i, 0)), _const_spec(w.shape),
                  pl.BlockSpec((tm, d), lambda i: (i, 0)), _const_spec((1, d))],
        out_specs=pl.BlockSpec((tm, d), lambda i: (i, 0)),
        out_shape=jax.ShapeDtypeStruct((t, d), F32),
        compiler_params=_params(("parallel",)),
        name="out_proj",
    )(y, w, x, g)


def _even_mixer_body(gate_ref, vb_ref, zs_ref, xc_ref, dt_ref, ws_ref, bs_ref, alog_ref, dskip_ref, gssd_ref,
                     e_ref, s0_ref, y_ref, sfin_ref, st_sc, *, blk, nsub):
    j = pl.program_id(1)

    @pl.when(j == 0)
    def _():
        st_sc[...] = s0_ref[...].T

    for sub in range(nsub):
        _mixer_block(slice(sub * blk, (sub + 1) * blk), blk, gate_ref, vb_ref, zs_ref, xc_ref, dt_ref, ws_ref,
                     bs_ref, alog_ref, dskip_ref, gssd_ref, e_ref, y_ref, st_sc)

    @pl.when(j == pl.num_programs(1) - 1)
    def _():
        sfin_ref[...] = st_sc[...].T


def _mixer_block(rs, blk, gate_ref, vb_ref, zs_ref, xc_ref, dt_ref, ws_ref, bs_ref, alog_ref, dskip_ref, gssd_ref,
                 e_ref, y_ref, st_sc):
    row = lax.broadcasted_iota(jnp.int32, (blk, blk), 0)
    col = lax.broadcasted_iota(jnp.int32, (blk, blk), 1)
    causal = col <= row

    chunk_causal = (col // SUB_CHUNK) <= (row // SUB_CHUNK)
    s_parts = []
    for g in range(A_GROUPS):
        wg = jnp.where(chunk_causal, ws_ref[g], 0.0).astype(BF16)
        sg = _dot(wg, vb_ref[rs, g * A_GROUP_DIM:(g + 1) * A_GROUP_DIM])
        s_parts.append(sg + bs_ref[:, g:g + 1])
    ya = gate_ref[rs, :] * jnp.concatenate(s_parts, axis=1)

    xs = xc_ref[rs, :B_WIDTH]
    bm = xc_ref[rs, B_WIDTH:B_WIDTH + B_GROUPS * B_STATE]
    cm = xc_ref[rs, B_WIDTH + B_GROUPS * B_STATE:]
    lane = lax.broadcasted_iota(jnp.int32, (1, LANES), 1)
    dt = dt_ref[rs, :]
    a = jnp.where(lane < B_HEADS, -jnp.exp(alog_ref[...]), 0.0)
    acs = _dot3_rhs(causal.astype(BF16), dt * a)
    acs_t = acs.T
    e = e_ref[...]
    dt_e = _dot3_lhs(dt, e)
    acs_e = _dot3_lhs(acs, e)
    expacs_e = jnp.exp(acs_e)
    toend_e = jnp.exp(acs_e[blk - 1:blk, :] - acs_e)
    blkdec_e = expacs_e[blk - 1:blk, :]
    dtx = dt_e * xs
    wds = (toend_e * dtx).astype(BF16)
    dtx_b = dtx.astype(BF16)
    lane_hp = lax.broadcasted_iota(jnp.int32, (blk, LANES), 1)
    gw = (B_HEADS // B_GROUPS) * B_HEAD_DIM
    ys_parts, yi_parts = [], []
    for g in range(B_GROUPS):
        bm_g = bm[:, g * B_STATE:(g + 1) * B_STATE]
        cm_g = cm[:, g * B_STATE:(g + 1) * B_STATE].astype(BF16)
        cb = _dot_nt(cm_g, bm_g.astype(BF16))
        st_g = st_sc[:, g * gw:(g + 1) * gw]
        ds = _dot(bm_g.T.astype(BF16), wds[:, g * gw:(g + 1) * gw])
        st_sc[:, g * gw:(g + 1) * gw] = blkdec_e[:, g * gw:(g + 1) * gw] * st_g + ds
        ys_parts.append(_dot(cm_g, st_g.astype(BF16)))
        for hp in range(gw // LANES):
            c0 = g * gw + hp * LANES
            rhs = dtx_b[:, c0:c0 + LANES]
            pair = None
            for half in range(2):
                h = c0 // B_HEAD_DIM + half
                seg = acs[:, h:h + 1] - acs_t[h:h + 1, :]
                decay = jnp.exp(jnp.where(causal, seg, -jnp.inf))
                m_h = (cb * decay).astype(BF16)
                keep = (lane_hp < B_HEAD_DIM) if half == 0 else (lane_hp >= B_HEAD_DIM)
                part = _dot(m_h, jnp.where(keep, rhs, jnp.zeros_like(rhs)))
                pair = part if pair is None else pair + part
            yi_parts.append(pair)
    y = (jnp.concatenate(yi_parts, axis=1) + jnp.concatenate(ys_parts, axis=1) * expacs_e
         + dskip_ref[...] * xs)
    y = y * zs_ref[rs, :]
    half_w = B_WIDTH // B_GROUPS
    yb = jnp.concatenate(
        [_rms(y[:, g * half_w:(g + 1) * half_w], gssd_ref[:, g * half_w:(g + 1) * half_w])
         for g in range(B_GROUPS)], axis=1)
    y_ref[rs, :] = jnp.concatenate([ya, yb], axis=1).astype(y_ref.dtype)


def even_mixer(gate, vb, zs, xc, dt, ws, bs_t, alog, dskip_e, gssd, e_mat, s0, blk):
    b, l, _ = gate.shape
    nsub = MIXER_BLOCKS_PER_STEP if l % (MIXER_BLOCKS_PER_STEP * blk) == 0 else 1
    tok = lambda w: pl.BlockSpec((None, nsub * blk, w), lambda i, j: (i, j, 0))
    hp = B_HEADS * B_HEAD_DIM
    state = pl.BlockSpec((None, hp, B_STATE), lambda i, j: (i, 0, 0))
    consts = [ws, bs_t, alog, dskip_e, gssd, e_mat]
    return pl.pallas_call(
        functools.partial(_even_mixer_body, blk=blk, nsub=nsub),
        grid=(b, l // (nsub * blk)),
        in_specs=[tok(A_WIDTH), tok(A_WIDTH), tok(B_WIDTH), tok(B_CONV_DIM), tok(LANES)]
        + [_const_spec(c.shape) for c in consts] + [state],
        out_specs=[tok(A_WIDTH + B_WIDTH), state],
        out_shape=[jax.ShapeDtypeStruct((b, l, A_WIDTH + B_WIDTH), BF16), jax.ShapeDtypeStruct((b, hp, B_STATE), F32)],
        scratch_shapes=[pltpu.VMEM((B_STATE, hp), F32)],
        compiler_params=_params(("parallel", "arbitrary")),
        name="even_mixer",
    )(gate, vb, zs, xc, dt, *consts, s0)


def _reset_at_stream_start(carry_sc):
    @pl.when(pl.program_id(1) == 0)
    def _():
        carry_sc[...] = jnp.zeros_like(carry_sc)


def _forget_cumsum(lf, carry_sc):
    n = lf.shape[0]
    lane = lax.broadcasted_iota(jnp.int32, (1, LANES), 1)
    lf = jnp.where(lane < C_HEADS, lf, 0.0)
    chunk = _largest_tile(n, 256, 8)
    tri = _tril(chunk).astype(BF16)
    carry, parts = carry_sc[...], []
    for c in range(n // chunk):
        part = _dot3_rhs(tri, lf[c * chunk:(c + 1) * chunk, :]) + carry
        carry = part[chunk - 1:chunk, :]
        parts.append(part)
    carry_sc[...] = carry
    return jnp.concatenate(parts, axis=0)


def _forget_lanes(fc):
    lane = lax.broadcasted_iota(jnp.int32, (1, LANES), 1)
    hi, mid, lo = _split3(fc * LOG2E)
    pieces = (hi.astype(F32) + pltpu.roll(mid.astype(F32), C_HEADS, axis=1)
              + pltpu.roll(lo.astype(F32), 2 * C_HEADS, axis=1))
    q_side = pieces + ((lane >= AUG_HALF) & (lane < 2 * AUG_HALF)).astype(F32)
    k_side = (lane < AUG_HALF).astype(F32) - pltpu.roll(pieces, AUG_HALF, axis=1)
    return q_side.astype(BF16), k_side.astype(BF16)


def _head_lanes(k_side, head):
    lane = lax.broadcasted_iota(jnp.int32, (1, LANES), 1)
    return jnp.where((lane % C_HEADS == head) & (lane < 2 * AUG_HALF), k_side, jnp.zeros_like(k_side))


def _fox_proj_body(x_ref, g_ref, wq_ref, wk_ref, wv_ref, wz_ref, wf_ref, gq_ref, gk_ref, bf_ref, *refs,
                   slot, first, with_aug):
    if not first:
        refs = refs[2:]
    q_ref, kf_ref, kb_ref, vf_ref, vb_ref, z_ref, lf_ref = refs[:7]
    tm = x_ref.shape[0]
    if with_aug:
        qa_ref, ka_ref, carry_sc = refs[7:]
        _reset_at_stream_start(carry_sc)
    if first:
        for s in range(kf_ref.shape[0]):
            if s != slot:
                kf_ref[s] = jnp.zeros(kf_ref.shape[1:], F32)
                vf_ref[s] = jnp.zeros(vf_ref.shape[1:], F32)
        kf_ref, vf_ref = kf_ref.at[slot], vf_ref.at[slot]
    h = _rms(x_ref[...], g_ref[...]).astype(BF16)
    q = _dot(h, wq_ref[...])
    k = _dot(h, wk_ref[...])
    v = _dot(h, wv_ref[...])
    q_scale = C_HEAD_DIM ** -0.5 * LOG2E
    kn = []
    for hd in range(C_HEADS):
        sl = slice(hd * C_HEAD_DIM, (hd + 1) * C_HEAD_DIM)
        q_ref[:, sl] = (_rms(q[:, sl], gq_ref[...]) * q_scale).astype(BF16)
        kn.append(_rms(k[:, sl], gk_ref[...]))
    kn = jnp.concatenate(kn, axis=1)
    kf_ref[...] = kn.reshape(tm, C_HEADS, C_HEAD_DIM)
    kb_ref[...] = kn.astype(BF16)
    vf_ref[...] = v.reshape(tm, C_HEADS, C_HEAD_DIM)
    vb_ref[...] = (v.T if with_aug else v).astype(BF16)
    z_ref[...] = _dot(h, wz_ref[...])
    lf = jax.nn.log_sigmoid(_dot(h, wf_ref[...]) + bf_ref[...])
    lf_ref[...] = lf
    if with_aug:
        qa_ref[...], ka_ref[...] = _forget_lanes(_forget_cumsum(lf, carry_sc))


def fox_proj(x, g, wq, wk, wv, wz, wf, gq, gk, bf, tm, n_slots, slot, prev_k, prev_v, with_aug):
    b, l, d = x.shape
    tok = lambda w: pl.BlockSpec((None, tm, w), lambda i, j: (i, j, 0))
    act = lambda w, dt: jax.ShapeDtypeStruct((b, l, w), dt)
    ins = [g, wq, wk, wv, wz, wf, gq, gk, bf]
    first = prev_k is None
    stacked = jax.ShapeDtypeStruct((n_slots, b, l, C_HEADS, C_HEAD_DIM), F32)
    if first:
        st_spec = pl.BlockSpec((n_slots, None, tm, C_HEADS, C_HEAD_DIM), lambda i, j: (0, i, j, 0, 0))
        extra, extra_specs, aliases = [], [], {}
    else:
        st_spec = pl.BlockSpec((None, None, tm, C_HEADS, C_HEAD_DIM), lambda i, j: (slot, i, j, 0, 0))
        extra, extra_specs = [prev_k, prev_v], [pl.BlockSpec(memory_space=pl.ANY)] * 2
        aliases = {1 + len(ins): 1, 2 + len(ins): 3}
    w = C_HEADS * C_HEAD_DIM
    aug_specs, aug_shapes, scratch = [], [], []
    v_spec, v_shape = tok(w), act(w, BF16)
    if with_aug:
        assert tm == VT_KEYS
        aug_specs, aug_shapes = [tok(LANES)] * 2, [act(LANES, BF16)] * 2
        scratch = [pltpu.VMEM((1, LANES), F32)]
        v_spec = pl.BlockSpec((None, None, w, tm), lambda i, j: (i, j, 0, 0))
        v_shape = jax.ShapeDtypeStruct((b, l // tm, w, tm), BF16)
    return pl.pallas_call(
        functools.partial(_fox_proj_body, slot=slot, first=first, with_aug=with_aug),
        grid=(b, l // tm),
        in_specs=[tok(d)] + [_const_spec(a.shape) for a in ins] + extra_specs,
        out_specs=[tok(w), st_spec, tok(w), st_spec, v_spec, tok(w), tok(LANES)] + aug_specs,
        out_shape=[act(w, BF16), stacked, act(w, BF16), stacked, v_shape, act(w, F32), act(LANES, F32)]
        + aug_shapes,
        input_output_aliases=aliases,
        scratch_shapes=scratch,
        compiler_params=_params(("parallel", "arbitrary")),
        name="fox_proj",
    )(x, *ins, *extra)


def _fox_aug_body(lf_ref, qa_ref, ka_ref, carry_sc):
    _reset_at_stream_start(carry_sc)
    qa_ref[...], ka_ref[...] = _forget_lanes(_forget_cumsum(lf_ref[...], carry_sc))


def fox_aug(lf, ts):
    b, s, _ = lf.shape
    return pl.pallas_call(
        _fox_aug_body,
        grid=(b, s // ts),
        in_specs=[pl.BlockSpec((None, ts, LANES), lambda i, j: (i, j, 0))],
        out_specs=[pl.BlockSpec((None, ts, LANES), lambda i, j: (i, j, 0))] * 2,
        out_shape=[jax.ShapeDtypeStruct((b, s, LANES), BF16)] * 2,
        scratch_shapes=[pltpu.VMEM((1, LANES), F32)],
        compiler_params=_params(("parallel", "arbitrary")),
        name="fox_aug",
    )(lf)


ATTN_ROWS = 1024
ATTN_CHAINS = 2


def _softmax_step(qc, carry, kc, vts, diagonal):
    m, l, acc = carry
    st = _dot_nt(kc, qc)
    if diagonal:
        row = lax.broadcasted_iota(jnp.int32, st.shape, 0)
        col = lax.broadcasted_iota(jnp.int32, st.shape, 1)
        st = jnp.where(row <= col, st, -jnp.inf)
    m_new = jnp.maximum(m, jnp.max(st, axis=0, keepdims=True))
    alpha = jnp.exp2(m - m_new)
    p = jnp.exp2(st - m_new)
    l = alpha * l + jnp.sum(p, axis=0, keepdims=True)
    pb = p.astype(BF16)
    pv = _dot(vts[0], pb[:VT_KEYS, :])
    for u in range(1, len(vts)):
        pv = pv + _dot(vts[u], pb[u * VT_KEYS:(u + 1) * VT_KEYS, :])
    return m_new, l, alpha * acc + pv


def _fox_attn_body(q_ref, qa_ref, k_ref, ka_ref, vt_ref, z_ref, o_ref, *, rows, chains, part):
    qi = pl.program_id(2)
    tq = rows * chains
    tiles = lambda n: n // VT_KEYS

    def queries(r0, n):
        return jnp.concatenate([q_ref[r0:r0 + n, :], qa_ref[r0:r0 + n, :]], axis=1)

    def keys(r0, n):
        return jnp.concatenate([k_ref[pl.ds(r0, n), :], _head_lanes(ka_ref[pl.ds(r0, n), :], pl.program_id(1))],
                               axis=1)

    qcs = [queries(c * rows, rows) for c in range(chains)]

    def below_diagonal(j, carries):
        kc = keys(pl.multiple_of(j * tq, tq), tq)
        vts = [vt_ref[j * tiles(tq) + u] for u in range(tiles(tq))]
        return tuple(_softmax_step(qcs[c], carries[c], kc, vts, False) for c in range(chains))

    init = tuple((jnp.full((1, rows), -jnp.inf, F32), jnp.zeros((1, rows), F32), jnp.zeros((C_HEAD_DIM, rows), F32))
                 for _ in range(chains))
    carries = lax.fori_loop(0, qi, below_diagonal, init)

    base = pl.multiple_of(qi * tq, tq)
    kc = keys(base, tq)
    vts = [vt_ref[qi * tiles(tq) + u] for u in range(tiles(tq))]
    for i in range(tq // part):
        c, o = divmod(i * part, rows)
        carry = tuple(a[:, o:o + part] for a in carries[c])
        qc = queries(i * part, part)
        if i > 0:
            carry = _softmax_step(qc, carry, kc[:i * part, :], vts[:tiles(i * part)], False)
        _, l, acc = _softmax_step(qc, carry, kc[i * part:(i + 1) * part, :],
                                  vts[tiles(i * part):tiles((i + 1) * part)], True)
        sl = slice(i * part, (i + 1) * part)
        o_ref[sl, :] = (_silu(z_ref[sl, :]) * (acc / l).T).astype(o_ref.dtype)


def fox_attn(q, qa, k, ka, vt, z):
    b, l, w = q.shape
    chains = ATTN_CHAINS if l % (ATTN_ROWS * ATTN_CHAINS) == 0 else 1
    rows = ATTN_ROWS if l % (ATTN_ROWS * chains) == 0 else l // chains
    tq = rows * chains
    part = rows
    qspec = pl.BlockSpec((None, tq, C_HEAD_DIM), lambda i, h, j: (i, j, h))
    qaspec = pl.BlockSpec((None, tq, LANES), lambda i, h, j: (i, j, 0))
    kspec = pl.BlockSpec((None, l, C_HEAD_DIM), lambda i, h, j: (i, 0, h))
    kaspec = pl.BlockSpec((None, l, LANES), lambda i, h, j: (i, 0, 0))
    vspec = pl.BlockSpec((None, l // VT_KEYS, C_HEAD_DIM, VT_KEYS), lambda i, h, j: (i, 0, h, 0))
    return pl.pallas_call(
        functools.partial(_fox_attn_body, rows=rows, chains=chains, part=part),
        grid=(b, C_HEADS, l // tq),
        in_specs=[qspec, qaspec, kspec, kaspec, vspec, qspec],
        out_specs=qspec,
        out_shape=jax.ShapeDtypeStruct((b, l, w), BF16),
        compiler_params=_params(("parallel", "parallel", "arbitrary")),
        name="fox_attn",
    )(q, qa, k, ka, vt, z)


def _fox_attn_sample_body(q_ref, qa_ref, kp_ref, kap_ref, vp_ref, kn_ref, kan_ref, vn_ref, z_ref, o_ref, *, p0):
    n = q_ref.shape[0]
    for hd in range(C_HEADS):
        sl = slice(hd * C_HEAD_DIM, (hd + 1) * C_HEAD_DIM)
        head_rows = pl.ds(hd, p0, stride=C_HEADS)
        qc = jnp.concatenate([q_ref[:, sl], qa_ref[...]], axis=1)
        s_past = _dot_nt(qc, jnp.concatenate([kp_ref[head_rows, :].astype(BF16), _head_lanes(kap_ref[...], hd)],
                                             axis=1))
        s_new = _dot_nt(qc, jnp.concatenate([kn_ref[:, sl], _head_lanes(kan_ref[...], hd)], axis=1))
        s_new = jnp.where(_tril(n), s_new, -jnp.inf)
        m = jnp.maximum(jnp.max(s_past, axis=-1, keepdims=True), jnp.max(s_new, axis=-1, keepdims=True))
        p_past = jnp.exp2(s_past - m)
        p_new = jnp.exp2(s_new - m)
        l = jnp.sum(p_past, axis=-1, keepdims=True) + jnp.sum(p_new, axis=-1, keepdims=True)
        acc = (_dot(p_past.astype(BF16), vp_ref[head_rows, :].astype(BF16))
               + _dot(p_new.astype(BF16), vn_ref[:, sl]))
        o_ref[:, sl] = (_silu(z_ref[:, sl]) * (acc / l)).astype(o_ref.dtype)


def fox_attn_sample(q, qa_all, cache_k, ka_all, cache_v, layer, k_new, v_new, z):
    b, n, w = q.shape
    p0 = cache_k.shape[2] // C_HEADS
    new = pl.BlockSpec((None, n, w), lambda i: (i, 0, 0))
    new_of_all = pl.BlockSpec((None, n, LANES), lambda i: (i, p0 // n, 0))
    past_aug = pl.BlockSpec((None, p0, LANES), lambda i: (i, 0, 0))
    past = pl.BlockSpec((None, None, p0 * C_HEADS, C_HEAD_DIM), lambda i: (layer, i, 0, 0))
    return pl.pallas_call(
        functools.partial(_fox_attn_sample_body, p0=p0),
        grid=(b,),
        in_specs=[new, new_of_all, past, past_aug, past, new, new_of_all, new, new],
        out_specs=new,
        out_shape=jax.ShapeDtypeStruct((b, n, w), BF16),
        compiler_params=_params(("parallel",)),
        name="fox_attn_sample",
    )(q, qa_all, cache_k, ka_all, cache_v, k_new, ka_all, v_new, z)


def _largest_tile(n, cap, mult):
    t = min(n, cap)
    while n % t or t % mult:
        t -= 1
    return t


def _pad_cols(a, width):
    return jnp.pad(a, ((0, 0), (0, width - a.shape[1])))


def _even_weights(w_in, w_out, ws, bs, gv, conv_w, conv_b, dt_bias, a_log, d_skip, g_ssd):
    cuts = (A_WIDTH, 2 * A_WIDTH, 3 * A_WIDTH, 3 * A_WIDTH + B_WIDTH, 3 * A_WIDTH + B_WIDTH + B_CONV_DIM)
    wu, wv, wza, wzb, wx, wdt = jnp.split(w_in, cuts, axis=1)
    e_mat = np.zeros((LANES, B_HEADS * B_HEAD_DIM), np.float32)
    for h in range(B_HEADS):
        e_mat[h, h * B_HEAD_DIM:(h + 1) * B_HEAD_DIM] = 1.0
    return dict(
        w_in=[p.astype(BF16) for p in (wu, wv, wza, wzb, wx, _pad_cols(wdt, LANES))], w_out=w_out.astype(BF16),
        ws=ws, bs=bs, gv=gv, cw=conv_w.T, cb=conv_b[None, :], dtb=_pad_cols(dt_bias[None, :], LANES),
        alog=_pad_cols(a_log[None, :], LANES), dskip_e=jnp.repeat(d_skip, B_HEAD_DIM)[None, :],
        gssd=g_ssd[None, :], e_mat=jnp.asarray(e_mat, BF16))


def _even_layer(x, g_pre, g_post, wts, conv0, s0, emit_v):
    b, l, d = x.shape
    t = b * l
    if l >= PROJ_ROWS:
        xin, tm, seg = x, _largest_tile(l, EVEN_PROJ_ROWS, 16), _largest_tile(l, EVEN_PROJ_ROWS, 16)
    else:
        xin, tm, seg = x.reshape(1, t, d), t, l
    outs = even_proj(xin, g_pre[None, :], wts["w_in"], wts["gv"], wts["cw"], wts["cb"], wts["dtb"],
                     conv0.reshape(xin.shape[0], tm // seg, B_CONV - 1, B_CONV_DIM), tm, seg, emit_v)
    gate, vb, *vrows, zs, xc, dt = (a.reshape(b, l, a.shape[2]) for a in outs[:-1])
    cfin = outs[-1].reshape(b, B_CONV - 1, B_CONV_DIM)
    blk = min(l, A_CHUNK)
    ycat, sfin = even_mixer(gate, vb, zs, xc, dt, wts["ws"][:, :blk, :blk], wts["bs"][:, :blk].T, wts["alog"],
                            wts["dskip_e"], wts["gssd"], wts["e_mat"],
                            s0.reshape(b, B_HEADS * B_HEAD_DIM, B_STATE), blk)
    xn = out_proj(ycat.reshape(t, ycat.shape[2]), wts["w_out"], x.reshape(t, d), g_post[None, :],
                  _largest_tile(t, OUT_PROJ_ROWS, 8))
    return (xn.reshape(b, l, d), *vrows, sfin.reshape(b, B_HEADS, B_HEAD_DIM, B_STATE), cfin)


def _odd_weights(w_in, b_f, w_out, gq, gk):
    w = C_HEADS * C_HEAD_DIM
    wq, wk, wv, wz, wf = jnp.split(w_in, (w, 2 * w, 3 * w, 4 * w), axis=1)
    return dict(wq=wq.astype(BF16), wk=wk.astype(BF16), wv=wv.astype(BF16), wz=wz.astype(BF16),
                wf=_pad_cols(wf, LANES).astype(BF16), bf=_pad_cols(b_f[None, :], LANES),
                w_out=w_out.astype(BF16), gq=gq[None, :], gk=gk[None, :])


def _odd_layer(x, g_pre, g_post, wts, n_slots, slot, prev_k, prev_v, cache):
    b, l, d = x.shape
    t = b * l
    prompt = cache is None
    xin = x if prompt else x.reshape(1, t, d)
    q, kst, kb, vst, vb, z, lf, *aug = fox_proj(
        xin, g_pre[None, :], wts["wq"], wts["wk"], wts["wv"], wts["wz"], wts["wf"], wts["gq"], wts["gk"],
        wts["bf"], _largest_tile(xin.shape[1], PROJ_ROWS, 16), n_slots, slot, prev_k, prev_v, prompt)
    if prompt:
        yc = fox_attn(q, aug[0], kb, aug[1], vb, z)
    else:
        q, kb, vb, z, lf = (a.reshape(b, l, a.shape[2]) for a in (q, kb, vb, z, lf))
        cache_k, cache_v, cache_logf = cache
        p0 = cache_k.shape[2]
        past_lf = _pad_cols(cache_logf[slot].reshape(b * p0, C_HEADS), LANES).reshape(b, p0, LANES)
        qa, ka = fox_aug(jnp.concatenate([past_lf, lf], axis=1), _largest_tile(p0 + l, 1040, 16))
        cache_rows = lambda c: c.reshape(c.shape[0], b, p0 * C_HEADS, C_HEAD_DIM)
        yc = fox_attn_sample(q, qa, cache_rows(cache_k), ka, cache_rows(cache_v), slot, kb, vb, z)
    xn = out_proj(yc.reshape(t, d), wts["w_out"], x.reshape(t, d), g_post[None, :],
                  _largest_tile(t, OUT_PROJ_ROWS, 8))
    return xn.reshape(b, l, d), kst, vst, lf[:, :, :C_HEADS]


def kernel(x_prompt, x_sample, cache_fox_k, cache_fox_v, cache_fox_logf, state_ssd, state_conv, norm_pre, norm_post, w_in_even, w_out_even, gmlp_ws, gmlp_bs, gmlp_gv, ssd_conv_w, ssd_conv_b, ssd_dt_bias, ssd_a_log, ssd_d, ssd_norm_g, w_in_odd, fox_b_forget, w_out_odd, fox_gq, fox_gk):
    xp, xs = x_prompt, x_sample
    bp = xp.shape[0]
    depth = norm_pre.shape[0]
    n_odd = depth // 2
    outs = {n: [] for n in ("lp", "sp", "cp", "ls", "ss", "cs", "gs")}
    kp = vp = ks = vs = None
    cache = (cache_fox_k, cache_fox_v, cache_fox_logf)
    for i in range(depth):
        j = i // 2
        if i % 2 == 0:
            wts = _even_weights(w_in_even[j], w_out_even[j], gmlp_ws[j], gmlp_bs[j], gmlp_gv[j], ssd_conv_w[j],
                                ssd_conv_b[j], ssd_dt_bias[j], ssd_a_log[j], ssd_d[j], ssd_norm_g[j])
            xp, sp, cp = _even_layer(xp, norm_pre[i], norm_post[i], wts,
                                     jnp.zeros((bp, B_CONV - 1, B_CONV_DIM), F32),
                                     jnp.zeros((bp, B_HEADS, B_HEAD_DIM, B_STATE), F32), False)
            xs, gs, ss, cs = _even_layer(xs, norm_pre[i], norm_post[i], wts, state_conv[j], state_ssd[j], True)
            for n, a in (("sp", sp), ("cp", cp), ("ss", ss), ("cs", cs), ("gs", gs)):
                outs[n].append(a)
        else:
            wts = _odd_weights(w_in_odd[j], fox_b_forget[j], w_out_odd[j], fox_gq[j], fox_gk[j])
            xp, kp, vp, lp = _odd_layer(xp, norm_pre[i], norm_post[i], wts, n_odd, j, kp, vp, None)
            xs, ks, vs, ls = _odd_layer(xs, norm_pre[i], norm_post[i], wts, n_odd, j, ks, vs, cache)
            outs["lp"].append(lp)
            outs["ls"].append(ls)
    st = {n: jnp.stack(a) for n, a in outs.items()}
    ks, vs = (a.reshape((n_odd,) + xs.shape[:2] + (C_HEADS, C_HEAD_DIM)) for a in (ks, vs))
    return (xp, xs, kp, vp, st["lp"], st["sp"], st["cp"], ks, vs, st["ls"], st["ss"], st["cs"], st["gs"])
```
